```python
import math
import jax, jax.numpy as jnp
from jax import lax
import numpy as np

D_MODEL = 4096
BATCH = 1
SEQ = 8192
DEPTH = 4

HEAD_DIM = 128
GRID_W = 64
Q_BLOCK = 128
NORM_EPS = 1e-6
NEG_INF = -1e30
A_HEADS = 16
A_KV_HEADS = 4
ROPE_THETA = 10000.0
B_HEADS = 16
NA_ROWS_MAX = 8
NA_COLS = 16
C_HEADS = 32
C_KV_HEADS = 8
C_WINDOW = 128

A_Q = A_HEADS * HEAD_DIM
A_KV = A_KV_HEADS * HEAD_DIM
B_W = B_HEADS * HEAD_DIM
AB_WIDTH = A_Q + B_W
AB_IN = A_Q + 2 * A_KV + 3 * B_W + AB_WIDTH
C_Q = C_HEADS * HEAD_DIM
C_KV = C_KV_HEADS * HEAD_DIM
C_IN = C_Q + 2 * C_KV + C_Q
N_EVEN = (DEPTH + 1) // 2
N_ODD = DEPTH // 2

kernel_name = "hybrid_gqa_natten_swa_gated_encoder"


def rms_norm(x, w):
    xf = x.astype(jnp.float32)
    y = xf * lax.rsqrt(jnp.mean(xf * xf, axis=-1, keepdims=True) + NORM_EPS)
    return (y * w.astype(jnp.float32)).astype(x.dtype)


def split_cols(z, sizes):
    idx = list(np.cumsum(sizes)[:-1])
    return jnp.split(z, idx, axis=-1)


def axial_rope(x, pos):
    half = HEAD_DIM // 2
    quarter = half // 2
    inv = jnp.exp(-math.log(ROPE_THETA) * jnp.arange(0, half, 2, dtype=jnp.float32) / half)
    rows = (pos // GRID_W).astype(jnp.float32)
    cols = (pos % GRID_W).astype(jnp.float32)

    def rot(xh, p):
        ang = p[:, None] * inv[None, :]
        cos = jnp.cos(ang)[None, :, None, :]
        sin = jnp.sin(ang)[None, :, None, :]
        xf = xh.astype(jnp.float32)
        x1, x2 = xf[..., :quarter], xf[..., quarter:]
        return jnp.concatenate([x1 * cos - x2 * sin, x2 * cos + x1 * sin], axis=-1)

    out = jnp.concatenate([rot(x[..., :half], rows), rot(x[..., half:], cols)], axis=-1)
    return out.astype(x.dtype)


def global_gqa(q, k, v):
    B, S, Hq, D = q.shape
    Hkv = k.shape[2]
    G = Hq // Hkv
    nb = S // Q_BLOCK
    scale = D ** -0.5
    qb = q.reshape(B, nb, Q_BLOCK, Hkv, G, D).transpose(1, 0, 2, 3, 4, 5)

    def one(qblk):
        s = jnp.einsum('bqkgd,bskd->bkgqs', qblk, k, preferred_element_type=jnp.float32) * scale
        p = jax.nn.softmax(s, axis=-1).astype(v.dtype)
        return jnp.einsum('bkgqs,bskd->bqkgd', p, v)

    o = lax.map(one, qb)
    return o.transpose(1, 0, 2, 3, 4, 5).reshape(B, S, Hq * D)


def neighbourhood_attention(q, k, v, rpb):
    B, S, H, D = q.shape
    rows = S // GRID_W
    wr = min(NA_ROWS_MAX, rows)
    wc = NA_COLS
    nb = S // Q_BLOCK
    scale = D ** -0.5
    qb = q.reshape(B, nb, Q_BLOCK, H, D).transpose(1, 0, 2, 3, 4)
    r_off = jnp.arange(wr, dtype=jnp.int32)
    c_off = jnp.arange(wc, dtype=jnp.int32)

    def one(args):
        blk, qblk = args
        t = blk * Q_BLOCK + jnp.arange(Q_BLOCK, dtype=jnp.int32)
        r = t // GRID_W
        c = t % GRID_W
        rs = jnp.clip(r - wr // 2, 0, rows - wr)
        cs = jnp.clip(c - wc // 2, 0, GRID_W - wc)
        kr = rs[:, None] + r_off[None, :]
        kc = cs[:, None] + c_off[None, :]
        idx = (kr[:, :, None] * GRID_W + kc[:, None, :]).reshape(Q_BLOCK, wr * wc)
        kg = k[:, idx]
        vg = v[:, idx]
        dr = kr - r[:, None] + (NA_ROWS_MAX - 1)
        dc = kc - c[:, None] + (NA_COLS - 1)
        bias = rpb[:, dr[:, :, None], dc[:, None, :]].reshape(H, Q_BLOCK, wr * wc)
        s = jnp.einsum('bqhd,bqnhd->bhqn', qblk, kg, preferred_element_type=jnp.float32) * scale
        s = s + bias.astype(jnp.float32)[None]
        p = jax.nn.softmax(s, axis=-1).astype(v.dtype)
        return jnp.einsum('bhqn,bqnhd->bqhd', p, vg)

    o = lax.map(one, (jnp.arange(nb, dtype=jnp.int32), qb))
    return o.transpose(1, 0, 2, 3, 4).reshape(B, S, H * D)


def alibi_slopes(n):
    return jnp.exp2(-8.0 * jnp.arange(1, n + 1, dtype=jnp.float32) / n)


def window_gqa_sink(q, k, v, sink):
    B, S, Hq, D = q.shape
    Hkv = k.shape[2]
    G = Hq // Hkv
    nb = S // Q_BLOCK
    W = C_WINDOW
    span = Q_BLOCK + 2 * W
    scale = D ** -0.5
    kp = jnp.pad(k, ((0, 0), (W, W), (0, 0), (0, 0)))
    vp = jnp.pad(v, ((0, 0), (W, W), (0, 0), (0, 0)))
    qb = q.reshape(B, nb, Q_BLOCK, Hkv, G, D).transpose(1, 0, 2, 3, 4, 5)
    slopes = alibi_slopes(Hq).reshape(Hkv, G)
    sink_g = sink.astype(jnp.float32).reshape(Hkv, G)

    def one(args):
        blk, qblk = args
        start = blk * Q_BLOCK
        kb = lax.dynamic_slice_in_dim(kp, start, span, axis=1)
        vb = lax.dynamic_slice_in_dim(vp, start, span, axis=1)
        t = start + jnp.arange(Q_BLOCK, dtype=jnp.int32)
        s_pos = start - W + jnp.arange(span, dtype=jnp.int32)
        dist = jnp.abs(t[:, None] - s_pos[None, :])
        valid = (dist <= W) & (s_pos >= 0)[None, :] & (s_pos < S)[None, :]
        sc = jnp.einsum('bqkgd,bskd->bkgqs', qblk, kb, preferred_element_type=jnp.float32) * scale
        sc = sc - slopes[None, :, :, None, None] * dist.astype(jnp.float32)[None, None, None]
        sc = jnp.where(valid[None, None, None], sc, NEG_INF)
        sink_col = jnp.broadcast_to(sink_g[None, :, :, None, None], (B, Hkv, G, Q_BLOCK, 1))
        p = jax.nn.softmax(jnp.concatenate([sc, sink_col], axis=-1), axis=-1)[..., :span]
        return jnp.einsum('bkgqs,bskd->bqkgd', p.astype(v.dtype), vb)

    o = lax.map(one, (jnp.arange(nb, dtype=jnp.int32), qb))
    return o.transpose(1, 0, 2, 3, 4, 5).reshape(B, S, Hq * D)


def setup_inputs(seed: int = 0) -> dict:
    key = jax.random.key(seed)
    ks = jax.random.split(key, 12)
    f32 = jnp.float32
    x = jax.random.normal(ks[0], (BATCH, SEQ, D_MODEL), f32)
    norm_w = 1.0 + 0.02 * jax.random.normal(ks[1], (DEPTH, D_MODEL), f32)
    w_in_ab = jax.random.normal(ks[2], (N_EVEN, D_MODEL, AB_IN), f32) * D_MODEL ** -0.5
    w_out_ab = jax.random.normal(ks[3], (N_EVEN, AB_WIDTH, D_MODEL), f32) * AB_WIDTH ** -0.5
    q_norm_a = 1.0 + 0.02 * jax.random.normal(ks[4], (N_EVEN, HEAD_DIM), f32)
    k_norm_a = 1.0 + 0.02 * jax.random.normal(ks[5], (N_EVEN, HEAD_DIM), f32)
    rpb_b = 0.1 * jax.random.normal(ks[6], (N_EVEN, B_HEADS, 2 * NA_ROWS_MAX - 1, 2 * NA_COLS - 1), f32)
    w_in_c = jax.random.normal(ks[7], (N_ODD, D_MODEL, C_IN), f32) * D_MODEL ** -0.5
    w_out_c = jax.random.normal(ks[8], (N_ODD, C_Q, D_MODEL), f32) * C_Q ** -0.5
    sink_c = 0.5 * jax.random.normal(ks[9], (N_ODD, C_HEADS), f32)
    final_norm_w = 1.0 + 0.02 * jax.random.normal(ks[10], (D_MODEL,), f32)
    return {"x": x, "norm_w": norm_w, "w_in_ab": w_in_ab, "w_out_ab": w_out_ab,
            "q_norm_a": q_norm_a, "k_norm_a": k_norm_a, "rpb_b": rpb_b,
            "w_in_c": w_in_c, "w_out_c": w_out_c, "sink_c": sink_c,
            "final_norm_w": final_norm_w}


def reference(x, norm_w, w_in_ab, w_out_ab, q_norm_a, k_norm_a, rpb_b, w_in_c, w_out_c, sink_c, final_norm_w):
    B, S, _ = x.shape
    pos = jnp.arange(S, dtype=jnp.int32)
    for layer in range(DEPTH):
        h = rms_norm(x, norm_w[layer])
        if layer % 2 == 0:
            e = layer // 2
            z = h @ w_in_ab[e]
            qa, ka, va, qb, kb, vb, gate = split_cols(z, [A_Q, A_KV, A_KV, B_W, B_W, B_W, AB_WIDTH])
            qa = qa.reshape(B, S, A_HEADS, HEAD_DIM)
            ka = ka.reshape(B, S, A_KV_HEADS, HEAD_DIM)
            va = va.reshape(B, S, A_KV_HEADS, HEAD_DIM)
            qa = axial_rope(rms_norm(qa, q_norm_a[e]), pos)
            ka = axial_rope(rms_norm(ka, k_norm_a[e]), pos)
            oa = global_gqa(qa, ka, va)
            ob = neighbourhood_attention(qb.reshape(B, S, B_HEADS, HEAD_DIM),
                                         kb.reshape(B, S, B_HEADS, HEAD_DIM),
                                         vb.reshape(B, S, B_HEADS, HEAD_DIM), rpb_b[e])
            y = jnp.concatenate([oa, ob], axis=-1) * jax.nn.silu(gate)
            x = x + y @ w_out_ab[e]
        else:
            o = layer // 2
            z = h @ w_in_c[o]
            qc, kc, vc, gate = split_cols(z, [C_Q, C_KV, C_KV, C_Q])
            oc = window_gqa_sink(qc.reshape(B, S, C_HEADS, HEAD_DIM),
                                 kc.reshape(B, S, C_KV_HEADS, HEAD_DIM),
                                 vc.reshape(B, S, C_KV_HEADS, HEAD_DIM), sink_c[o])
            x = x + (oc * jax.nn.silu(gate)) @ w_out_c[o]
    return rms_norm(x, final_norm_w)
```

```python
import functools
import math
from typing import NamedTuple

import jax
import jax.numpy as jnp
import numpy as np
from jax import lax
from jax.experimental import pallas as pl
from jax.experimental.pallas import tpu as pltpu

F32 = jnp.float32
BF16 = jnp.bfloat16

LANES = 128
NORM_EPS = 1e-6
NEG_INF = -1e30
ROPE_THETA = 10000.0
GRID_W = 64
NA_ROWS = 8
NA_COLS = 16
NA_UNION_ROWS = 10
C_WINDOW = 128
VMEM_LIMIT_BYTES = 56 * 1024 * 1024


class Config(NamedTuple):
    a_heads: int = 16
    a_kv_heads: int = 4
    b_heads: int = 16
    c_heads: int = 32
    c_kv_heads: int = 8
    tm: int = 1024
    tn: int = 512
    tq_a: int = 256
    tk_a: int = 1024
    tq_c: int = 256
    tq_b: int = 1024
    t_norm: int = 256


def _params(*sem):
    return pltpu.CompilerParams(dimension_semantics=sem, vmem_limit_bytes=VMEM_LIMIT_BYTES)


def _rmsnorm_kernel(x_ref, w_ref, o_ref):
    x = x_ref[...]
    ms = jnp.mean(x * x, axis=-1, keepdims=True)
    o_ref[...] = (x * lax.rsqrt(ms + NORM_EPS) * w_ref[...]).astype(o_ref.dtype)


def _rmsnorm(x, w, out_dtype, t):
    m, d = x.shape
    return pl.pallas_call(
        _rmsnorm_kernel,
        grid=(m // t,),
        in_specs=[pl.BlockSpec((t, d), lambda i: (i, 0)),
                  pl.BlockSpec((1, d), lambda i: (0, 0))],
        out_specs=pl.BlockSpec((t, d), lambda i: (i, 0)),
        out_shape=jax.ShapeDtypeStruct((m, d), out_dtype),
        compiler_params=_params("parallel"),
        name="rmsnorm",
    )(x, w.reshape(1, d))


def _stage_weight(w_ref, wbf_ref):
    @pl.when(pl.program_id(1) == 0)
    def _():
        wbf_ref[...] = w_ref[...].astype(BF16)


def _norm_rope(z, w, cos, sin_lo, sin_hi):
    ms = jnp.mean(z * z, axis=-1, keepdims=True)
    y = z * lax.rsqrt(ms + NORM_EPS) * w
    quarter = LANES // 4
    return (y * cos + pltpu.roll(y, LANES - quarter, 1) * sin_lo
            + pltpu.roll(y, quarter, 1) * sin_hi)


def _inproj_kernel(*refs, regions, rope):
    if rope:
        h_ref, w_ref, cos_ref, slo_ref, shi_ref, qn_ref, kn_ref, o_ref, wbf_ref = refs
    else:
        h_ref, w_ref, o_ref, wbf_ref = refs
    _stage_weight(w_ref, wbf_ref)
    j = pl.program_id(0)
    acc = jnp.dot(h_ref[...], wbf_ref[...], preferred_element_type=F32)
    tn = acc.shape[1]
    for lo, hi, kind, scale in regions:
        @pl.when((j >= lo) & (j < hi))
        def _(kind=kind, scale=scale):
            if kind == "cast":
                o_ref[...] = (acc * scale if scale != 1.0 else acc).astype(o_ref.dtype)
            else:
                nw = (qn_ref if kind == "q_rope" else kn_ref)[...]
                cos, slo, shi = cos_ref[...], slo_ref[...], shi_ref[...]
                for hh in range(tn // LANES):
                    sl = slice(hh * LANES, (hh + 1) * LANES)
                    y = _norm_rope(acc[:, sl], nw, cos, slo, shi)
                    o_ref[:, sl] = (y * scale if scale != 1.0 else y).astype(o_ref.dtype)


def _inproj(h, w, e, regions, n_out, cfg, rope_args=None):
    m, k = h.shape
    tm, tn = min(cfg.tm, m), cfg.tn
    row = lambda j, i: (i, 0)
    in_specs = [pl.BlockSpec((tm, k), row),
                pl.BlockSpec((None, k, tn), lambda j, i: (e, 0, j))]
    args = [h, w]
    if rope_args is not None:
        cos, slo, shi, qn, kn = rope_args
        in_specs += [pl.BlockSpec((tm, LANES), row)] * 3
        in_specs += [pl.BlockSpec((1, LANES), lambda j, i: (0, 0))] * 2
        args += [cos, slo, shi, qn.reshape(1, LANES), kn.reshape(1, LANES)]
    tiles = tuple((lo // tn, hi // tn, kind, scale) for lo, hi, kind, scale in regions)
    return pl.pallas_call(
        functools.partial(_inproj_kernel, regions=tiles, rope=rope_args is not None),
        grid=(n_out // tn, m // tm),
        in_specs=in_specs,
        out_specs=pl.BlockSpec((tm, tn), lambda j, i: (i, j)),
        out_shape=jax.ShapeDtypeStruct((m, n_out), BF16),
        scratch_shapes=[pltpu.VMEM((k, tn), BF16)],
        compiler_params=_params("parallel", "arbitrary"),
        name="inproj",
    )(*args)


def _gate_kernel(h_ref, w_ref, o_ref, wbf_ref):
    _stage_weight(w_ref, wbf_ref)
    z = jnp.dot(h_ref[...], wbf_ref[...], preferred_element_type=F32)
    o_ref[...] = z / (1.0 + jnp.exp(-z))


def _gate_proj(h, w, e, col0, n_out, cfg):
    m, k = h.shape
    tm, tn = min(cfg.tm, m), cfg.tn
    off = col0 // tn
    return pl.pallas_call(
        _gate_kernel,
        grid=(n_out // tn, m // tm),
        in_specs=[pl.BlockSpec((tm, k), lambda j, i: (i, 0)),
                  pl.BlockSpec((None, k, tn), lambda j, i: (e, 0, j + off))],
        out_specs=pl.BlockSpec((tm, tn), lambda j, i: (i, j)),
        out_shape=jax.ShapeDtypeStruct((m, n_out), F32),
        scratch_shapes=[pltpu.VMEM((k, tn), BF16)],
        compiler_params=_params("parallel", "arbitrary"),
        name="gateproj",
    )(h, w)


def _outproj_kernel(*refs, n_lhs):
    y_refs = refs[:n_lhs]
    w_ref, x_ref, o_ref, wbf_ref = refs[n_lhs:]
    _stage_weight(w_ref, wbf_ref)
    acc = x_ref[...]
    k0 = 0
    for y_ref in y_refs:
        kk = y_ref.shape[1]
        acc = acc + jnp.dot(y_ref[...], wbf_ref[k0:k0 + kk, :], preferred_element_type=F32)
        k0 += kk
    o_ref[...] = acc


def _outproj(ys, w, e, x, cfg):
    m, n = x.shape
    k = w.shape[1]
    tm, tn = min(cfg.tm, m), cfg.tn
    in_specs = [pl.BlockSpec((tm, y.shape[1]), lambda j, i: (i, 0)) for y in ys]
    in_specs += [pl.BlockSpec((None, k, tn), lambda j, i: (e, 0, j)),
                 pl.BlockSpec((tm, tn), lambda j, i: (i, j))]
    return pl.pallas_call(
        functools.partial(_outproj_kernel, n_lhs=len(ys)),
        grid=(n // tn, m // tm),
        in_specs=in_specs,
        out_specs=pl.BlockSpec((tm, tn), lambda j, i: (i, j)),
        out_shape=jax.ShapeDtypeStruct((m, n), F32),
        scratch_shapes=[pltpu.VMEM((k, tn), BF16)],
        compiler_params=_params("parallel", "arbitrary"),
        name="outproj",
    )(*ys, w, x)


def _attn_a_kernel(q_ref, k_ref, v_ref, g_ref, o_ref, *, tk, group):
    tq = q_ref.shape[0]
    nk = k_ref.shape[0] // tk
    for g in range(group):
        sl = slice(g * LANES, (g + 1) * LANES)
        q = q_ref[:, sl]

        def body(t, carry, q=q):
            m, l, acc = carry
            off = pl.multiple_of(t * tk, tk)
            k = k_ref[pl.ds(off, tk), :]
            v = v_ref[pl.ds(off, tk), :]
            s = lax.dot_general(q, k, (((1,), (1,)), ((), ())), preferred_element_type=F32)
            m_new = jnp.maximum(m, jnp.max(s, axis=-1, keepdims=True))
            p = jnp.exp(s - m_new)
            alpha = jnp.exp(m - m_new)
            l = alpha * l + jnp.sum(p, axis=-1, keepdims=True)
            acc = alpha * acc + jnp.dot(p.astype(BF16), v, preferred_element_type=F32)
            return m_new, l, acc

        init = (jnp.full((tq, 1), NEG_INF, F32), jnp.zeros((tq, 1), F32),
                jnp.zeros((tq, LANES), F32))
        _, l, acc = lax.fori_loop(0, nk, body, init)
        o_ref[:, sl] = (acc / l * g_ref[:, sl]).astype(o_ref.dtype)


def _attn_a(z, gate, cfg, q_col0, k_col0, v_col0):
    s_len = z.shape[0]
    group = cfg.a_heads // cfg.a_kv_heads
    gw = group * LANES
    tq = min(cfg.tq_a, s_len)
    tk = min(cfg.tk_a, s_len)
    return pl.pallas_call(
        functools.partial(_attn_a_kernel, tk=tk, group=group),
        grid=(cfg.a_kv_heads, s_len // tq),
        in_specs=[pl.BlockSpec((tq, gw), lambda kv, i: (i, q_col0 // gw + kv)),
                  pl.BlockSpec((s_len, LANES), lambda kv, i: (0, k_col0 // LANES + kv)),
                  pl.BlockSpec((s_len, LANES), lambda kv, i: (0, v_col0 // LANES + kv)),
                  pl.BlockSpec((tq, gw), lambda kv, i: (i, kv))],
        out_specs=pl.BlockSpec((tq, gw), lambda kv, i: (i, kv)),
        out_shape=jax.ShapeDtypeStruct((s_len, cfg.a_heads * LANES), BF16),
        compiler_params=_params("parallel", "arbitrary"),
        name="attn_global",
    )(z, z, z, gate)


def _natten_tables(rpb, rows):
    r0 = np.array([4, 0, 2, rows - 4, rows - 2])
    u0 = np.clip(r0 - NA_ROWS // 2, 0, rows - NA_UNION_ROWS)
    r = r0[:, None] + np.arange(2)[None, :]
    rs = np.clip(r - NA_ROWS // 2, 0, rows - NA_ROWS)
    kr = u0[:, None] + np.arange(NA_UNION_ROWS)[None, :]
    vr = (kr[:, None, :] >= rs[:, :, None]) & (kr[:, None, :] < rs[:, :, None] + NA_ROWS)
    dr = np.clip(kr[:, None, :] - r[:, :, None] + (NA_ROWS - 1), 0, 2 * NA_ROWS - 2)
    c = np.arange(GRID_W)
    cs = np.clip(c - NA_COLS // 2, 0, GRID_W - NA_COLS)
    vc = (c[None, :] >= cs[:, None]) & (c[None, :] < cs[:, None] + NA_COLS)
    dc = np.clip(c[None, :] - c[:, None] + (NA_COLS - 1), 0, 2 * NA_COLS - 2)
    bias = rpb[:, dr[:, :, None, :, None], dc[None, None, :, None, :]]
    valid = vr[:, :, None, :, None] & vc[None, None, :, None, :]
    tbl = jnp.where(valid[None], bias.astype(F32), NEG_INF)
    h = rpb.shape[0]
    return tbl.transpose(1, 0, 2, 3, 4, 5).reshape(5, h, 2 * GRID_W, NA_UNION_ROWS * GRID_W)


def _natten_kernel(q_ref, k_ref, v_ref, g_ref, tbl_ref, o_ref, *, rows):
    qb = 2 * GRID_W
    span = NA_UNION_ROWS * GRID_W
    nb_step = q_ref.shape[0] // qb
    nb = rows // 2
    step = pl.program_id(1)

    def body(b, carry):
        blk = step * nb_step + b
        u0 = jnp.clip(2 * blk - NA_ROWS // 2, 0, rows - NA_UNION_ROWS)
        typ = jnp.where(blk == 0, 1, jnp.where(blk == 1, 2,
              jnp.where(blk == nb - 2, 3, jnp.where(blk == nb - 1, 4, 0))))
        qoff = pl.multiple_of(b * qb, qb)
        koff = pl.multiple_of(u0 * GRID_W, GRID_W)
        q = q_ref[pl.ds(qoff, qb), :]
        k = k_ref[pl.ds(koff, span), :]
        v = v_ref[pl.ds(koff, span), :]
        s = lax.dot_general(q, k, (((1,), (1,)), ((), ())), preferred_element_type=F32)
        s = s + tbl_ref[typ, 0]
        m = jnp.max(s, axis=-1, keepdims=True)
        e = jnp.exp(s - m)
        l = jnp.sum(e, axis=-1, keepdims=True)
        o = jnp.dot(e.astype(BF16), v, preferred_element_type=F32)
        o_ref[pl.ds(qoff, qb), :] = (o / l * g_ref[pl.ds(qoff, qb), :]).astype(o_ref.dtype)
        return carry

    lax.fori_loop(0, nb_step, body, 0)


def _natten(z, gate, tbl, cfg, q_col0, k_col0, v_col0, gate_col0):
    s_len = z.shape[0]
    rows = s_len // GRID_W
    tq = min(cfg.tq_b, s_len)
    span = NA_UNION_ROWS * GRID_W
    return pl.pallas_call(
        functools.partial(_natten_kernel, rows=rows),
        grid=(cfg.b_heads, s_len // tq),
        in_specs=[pl.BlockSpec((tq, LANES), lambda h, i: (i, q_col0 // LANES + h)),
                  pl.BlockSpec((s_len, LANES), lambda h, i: (0, k_col0 // LANES + h)),
                  pl.BlockSpec((s_len, LANES), lambda h, i: (0, v_col0 // LANES + h)),
                  pl.BlockSpec((tq, LANES), lambda h, i: (i, gate_col0 // LANES + h)),
                  pl.BlockSpec((5, 1, 2 * GRID_W, span), lambda h, i: (0, h, 0, 0))],
        out_specs=pl.BlockSpec((tq, LANES), lambda h, i: (i, h)),
        out_shape=jax.ShapeDtypeStruct((s_len, cfg.b_heads * LANES), BF16),
        compiler_params=_params("parallel", "arbitrary"),
        name="attn_neighbourhood",
    )(z, z, z, gate, tbl)


def _attn_c_kernel(slope_ref, sink_ref, q_ref, k_ref, v_ref, g_ref, o_ref, *, group):
    tq = q_ref.shape[0]
    s_len = k_ref.shape[0]
    span = tq + 2 * C_WINDOW
    kv = pl.program_id(0)
    t0 = pl.program_id(1) * tq
    start = pl.multiple_of(jnp.clip(t0 - C_WINDOW, 0, s_len - span), C_WINDOW)
    k = k_ref[pl.ds(start, span), :]
    v = v_ref[pl.ds(start, span), :]
    rel = (lax.broadcasted_iota(jnp.int32, (tq, span), 0)
           - lax.broadcasted_iota(jnp.int32, (tq, span), 1) + (t0 - start))
    dist = jnp.abs(rel)
    valid = dist <= C_WINDOW
    distf = dist.astype(F32)
    for g in range(group):
        head = kv * group + g
        sl = slice(g * LANES, (g + 1) * LANES)
        s = lax.dot_general(q_ref[:, sl], k, (((1,), (1,)), ((), ())), preferred_element_type=F32)
        s = jnp.where(valid, s - slope_ref[head] * distf, NEG_INF)
        sink = sink_ref[head]
        m = jnp.maximum(jnp.max(s, axis=-1, keepdims=True), sink)
        e = jnp.exp(s - m)
        l = jnp.sum(e, axis=-1, keepdims=True) + jnp.exp(sink - m)
        o = jnp.dot(e.astype(BF16), v, preferred_element_type=F32)
        o_ref[:, sl] = (o / l * g_ref[:, sl]).astype(o_ref.dtype)


def _attn_c(z, gate, slopes, sink, cfg, q_col0, k_col0, v_col0):
    s_len = z.shape[0]
    group = cfg.c_heads // cfg.c_kv_heads
    gw = group * LANES
    tq = cfg.tq_c
    assert s_len >= tq + 2 * C_WINDOW and s_len % tq == 0
    smem = pl.BlockSpec(memory_space=pltpu.SMEM)
    return pl.pallas_call(
        functools.partial(_attn_c_kernel, group=group),
        grid=(cfg.c_kv_heads, s_len // tq),
        in_specs=[smem, smem,
                  pl.BlockSpec((tq, gw), lambda kv, i: (i, q_col0 // gw + kv)),
                  pl.BlockSpec((s_len, LANES), lambda kv, i: (0, k_col0 // LANES + kv)),
                  pl.BlockSpec((s_len, LANES), lambda kv, i: (0, v_col0 // LANES + kv)),
                  pl.BlockSpec((tq, gw), lambda kv, i: (i, kv))],
        out_specs=pl.BlockSpec((tq, gw), lambda kv, i: (i, kv)),
        out_shape=jax.ShapeDtypeStruct((s_len, cfg.c_heads * LANES), BF16),
        compiler_params=_params("parallel", "arbitrary"),
        name="attn_window",
    )(slopes, sink, z, z, z, gate)


def _rope_tables(s_len):
    half = LANES // 2
    quarter = half // 2
    inv = jnp.exp(-math.log(ROPE_THETA) * jnp.arange(0, half, 2, dtype=F32) / half)
    pos = jnp.arange(s_len, dtype=jnp.int32)
    rows = (pos // GRID_W).astype(F32)
    cols = (pos % GRID_W).astype(F32)
    ang = jnp.concatenate([rows[:, None] * inv[None, :]] * 2 + [cols[:, None] * inv[None, :]] * 2, axis=1)
    first = (np.arange(LANES) % half) < quarter
    cos, sin = jnp.cos(ang), jnp.sin(ang)
    return cos, jnp.where(first[None], -sin, 0.0), jnp.where(first[None], 0.0, sin)


def _forward(x, norm_w, w_in_ab, w_out_ab, q_norm_a, k_norm_a, rpb_b, w_in_c, w_out_c, sink_c,
             final_norm_w, cfg):
    bsz, s_len, d = x.shape
    assert bsz == 1 and s_len % (2 * GRID_W) == 0 and s_len // GRID_W >= NA_UNION_ROWS
    depth = norm_w.shape[0]
    scale = LANES ** -0.5
    a_q, a_kv, b_w = cfg.a_heads * LANES, cfg.a_kv_heads * LANES, cfg.b_heads * LANES
    c_q, c_kv = cfg.c_heads * LANES, cfg.c_kv_heads * LANES
    rope = _rope_tables(s_len)
    slopes = jnp.exp2(-8.0 * jnp.arange(1, cfg.c_heads + 1, dtype=F32) / cfg.c_heads)
    xs = x.reshape(s_len, d)
    for layer in range(depth):
        h = _rmsnorm(xs, norm_w[layer], BF16, cfg.t_norm)
        if layer % 2 == 0:
            e = layer // 2
            ka0, va0 = a_q, a_q + a_kv
            qb0 = va0 + a_kv
            kb0, vb0 = qb0 + b_w, qb0 + 2 * b_w
            g0 = vb0 + b_w
            regions = ((0, ka0, "q_rope", scale), (ka0, va0, "k_rope", 1.0), (va0, qb0, "cast", 1.0),
                       (qb0, kb0, "cast", scale), (kb0, g0, "cast", 1.0))
            z = _inproj(h, w_in_ab, e, regions, g0, cfg, rope + (q_norm_a[e], k_norm_a[e]))
            gate = _gate_proj(h, w_in_ab, e, g0, a_q + b_w, cfg)
            ya = _attn_a(z, gate, cfg, 0, ka0, va0)
            yb = _natten(z, gate, _natten_tables(rpb_b[e], s_len // GRID_W), cfg, qb0, kb0, vb0, a_q)
            xs = _outproj([ya, yb], w_out_ab, e, xs, cfg)
        else:
            o = layer // 2
            kc0, vc0 = c_q, c_q + c_kv
            g0 = vc0 + c_kv
            regions = ((0, kc0, "cast", scale), (kc0, g0, "cast", 1.0))
            z = _inproj(h, w_in_c, o, regions, g0, cfg)
            gate = _gate_proj(h, w_in_c, o, g0, c_q, cfg)
            y = _attn_c(z, gate, slopes, sink_c[o].astype(F32), cfg, 0, kc0, vc0)
            xs = _outproj([y], w_out_c, o, xs, cfg)
    out = _rmsnorm(xs, final_norm_w, x.dtype, cfg.t_norm)
    return out.reshape(bsz, s_len, d)


def kernel(x, norm_w, w_in_ab, w_out_ab, q_norm_a, k_norm_a, rpb_b, w_in_c, w_out_c, sink_c, final_norm_w):
    return _forward(x, norm_w, w_in_ab, w_out_ab, q_norm_a, k_norm_a, rpb_b, w_in_c, w_out_c, sink_c,
                    final_norm_w, Config())
```

```python
import functools
import math
from typing import NamedTuple

import jax
import jax.numpy as jnp
import numpy as np
from jax import lax
from jax.experimental import pallas as pl
from jax.experimental.pallas import tpu as pltpu

F32 = jnp.float32
BF16 = jnp.bfloat16

LANES = 128
NORM_EPS = 1e-6
NEG_INF = -1e30
LOG2E = math.log2(math.e)
ROPE_THETA = 10000.0
GRID_W = 64
NA_ROWS = 8
NA_COLS = 16
NA_UNION_ROWS = 10
C_WINDOW = 128
VMEM_LIMIT_BYTES = 56 * 1024 * 1024


class Config(NamedTuple):
    a_heads: int = 16
    a_kv_heads: int = 4
    b_heads: int = 16
    c_heads: int = 32
    c_kv_heads: int = 8
    tm: int = 1024
    tn: int = 512
    tq_a: int = 256
    tk_a: int = 512
    unroll_a: int = 4
    tq_c: int = 256
    tq_b: int = 1024
    unroll_b: int = 8
    t_norm: int = 256


def _params(*sem):
    return pltpu.CompilerParams(dimension_semantics=sem, vmem_limit_bytes=VMEM_LIMIT_BYTES)


def _rmsnorm_kernel(x_ref, w_ref, o_ref):
    x = x_ref[...]
    ms = jnp.mean(x * x, axis=-1, keepdims=True)
    o_ref[...] = (x * lax.rsqrt(ms + NORM_EPS) * w_ref[...]).astype(o_ref.dtype)


def _rmsnorm(x, w, out_dtype, t):
    m, d = x.shape
    return pl.pallas_call(
        _rmsnorm_kernel,
        grid=(m // t,),
        in_specs=[pl.BlockSpec((t, d), lambda i: (i, 0)),
                  pl.BlockSpec((1, d), lambda i: (0, 0))],
        out_specs=pl.BlockSpec((t, d), lambda i: (i, 0)),
        out_shape=jax.ShapeDtypeStruct((m, d), out_dtype),
        compiler_params=_params("parallel"),
        name="rmsnorm",
    )(x, w.reshape(1, d))


def _stage_weight(w_ref, wbf_ref):
    @pl.when(pl.program_id(1) == 0)
    def _():
        wbf_ref[...] = w_ref[...].astype(BF16)


def _norm_rope(z, w, cos, sin_lo, sin_hi):
    ms = jnp.mean(z * z, axis=-1, keepdims=True)
    y = z * lax.rsqrt(ms + NORM_EPS) * w
    quarter = LANES // 4
    return (y * cos + pltpu.roll(y, LANES - quarter, 1) * sin_lo
            + pltpu.roll(y, quarter, 1) * sin_hi)


def _inproj_kernel(*refs, regions, rope):
    if rope:
        h_ref, w_ref, cos_ref, slo_ref, shi_ref, qn_ref, kn_ref, o_ref, wbf_ref = refs
    else:
        h_ref, w_ref, o_ref, wbf_ref = refs
    _stage_weight(w_ref, wbf_ref)
    j = pl.program_id(0)
    acc = jnp.dot(h_ref[...], wbf_ref[...], preferred_element_type=F32)
    tn = acc.shape[1]
    for lo, hi, kind, scale in regions:
        @pl.when((j >= lo) & (j < hi))
        def _(kind=kind, scale=scale):
            if kind == "cast":
                o_ref[...] = (acc * scale if scale != 1.0 else acc).astype(o_ref.dtype)
            else:
                nw = (qn_ref if kind == "q_rope" else kn_ref)[...]
                cos, slo, shi = cos_ref[...], slo_ref[...], shi_ref[...]
                for hh in range(tn // LANES):
                    sl = slice(hh * LANES, (hh + 1) * LANES)
                    y = _norm_rope(acc[:, sl], nw, cos, slo, shi)
                    o_ref[:, sl] = (y * scale if scale != 1.0 else y).astype(o_ref.dtype)


def _inproj(h, w, e, regions, n_out, cfg, rope_args=None):
    m, k = h.shape
    tm, tn = min(cfg.tm, m), cfg.tn
    row = lambda j, i: (i, 0)
    in_specs = [pl.BlockSpec((tm, k), row),
                pl.BlockSpec((None, k, tn), lambda j, i: (e, 0, j))]
    args = [h, w]
    if rope_args is not None:
        cos, slo, shi, qn, kn = rope_args
        in_specs += [pl.BlockSpec((tm, LANES), row)] * 3
        in_specs += [pl.BlockSpec((1, LANES), lambda j, i: (0, 0))] * 2
        args += [cos, slo, shi, qn.reshape(1, LANES), kn.reshape(1, LANES)]
    tiles = tuple((lo // tn, hi // tn, kind, scale) for lo, hi, kind, scale in regions)
    return pl.pallas_call(
        functools.partial(_inproj_kernel, regions=tiles, rope=rope_args is not None),
        grid=(n_out // tn, m // tm),
        in_specs=in_specs,
        out_specs=pl.BlockSpec((tm, tn), lambda j, i: (i, j)),
        out_shape=jax.ShapeDtypeStruct((m, n_out), BF16),
        scratch_shapes=[pltpu.VMEM((k, tn), BF16)],
        compiler_params=_params("parallel", "arbitrary"),
        name="inproj",
    )(*args)


def _gate_kernel(h_ref, w_ref, o_ref, wbf_ref):
    _stage_weight(w_ref, wbf_ref)
    z = jnp.dot(h_ref[...], wbf_ref[...], preferred_element_type=F32)
    o_ref[...] = z / (1.0 + jnp.exp(-z))


def _gate_proj(h, w, e, col0, n_out, cfg):
    m, k = h.shape
    tm, tn = min(cfg.tm, m), cfg.tn
    off = col0 // tn
    return pl.pallas_call(
        _gate_kernel,
        grid=(n_out // tn, m // tm),
        in_specs=[pl.BlockSpec((tm, k), lambda j, i: (i, 0)),
                  pl.BlockSpec((None, k, tn), lambda j, i: (e, 0, j + off))],
        out_specs=pl.BlockSpec((tm, tn), lambda j, i: (i, j)),
        out_shape=jax.ShapeDtypeStruct((m, n_out), F32),
        scratch_shapes=[pltpu.VMEM((k, tn), BF16)],
        compiler_params=_params("parallel", "arbitrary"),
        name="gateproj",
    )(h, w)


def _outproj_kernel(*refs, n_lhs):
    y_refs = refs[:n_lhs]
    w_ref, x_ref, o_ref, wbf_ref = refs[n_lhs:]
    _stage_weight(w_ref, wbf_ref)
    acc = x_ref[...]
    k0 = 0
    for y_ref in y_refs:
        kk = y_ref.shape[1]
        acc = acc + jnp.dot(y_ref[...], wbf_ref[k0:k0 + kk, :], preferred_element_type=F32)
        k0 += kk
    o_ref[...] = acc


def _outproj(ys, w, e, x, cfg):
    m, n = x.shape
    k = w.shape[1]
    tm, tn = min(cfg.tm, m), cfg.tn
    in_specs = [pl.BlockSpec((tm, y.shape[1]), lambda j, i: (i, 0)) for y in ys]
    in_specs += [pl.BlockSpec((None, k, tn), lambda j, i: (e, 0, j)),
                 pl.BlockSpec((tm, tn), lambda j, i: (i, j))]
    return pl.pallas_call(
        functools.partial(_outproj_kernel, n_lhs=len(ys)),
        grid=(n // tn, m // tm),
        in_specs=in_specs,
        out_specs=pl.BlockSpec((tm, tn), lambda j, i: (i, j)),
        out_shape=jax.ShapeDtypeStruct((m, n), F32),
        scratch_shapes=[pltpu.VMEM((k, tn), BF16)],
        compiler_params=_params("parallel", "arbitrary"),
        name="outproj",
    )(*ys, w, x)


def _attn_a_kernel(q_ref, k_ref, v_ref, g_ref, o_ref, qs_ref, *, tk, group, unroll):
    tq = q_ref.shape[0]
    nk = k_ref.shape[0] // tk
    for g in range(group):
        qs_ref[g * tq:(g + 1) * tq, :] = q_ref[:, g * LANES:(g + 1) * LANES]
    q = qs_ref[...]

    def body(t, carry):
        m, l, acc = carry
        off = pl.multiple_of(t * tk, tk)
        k = k_ref[pl.ds(off, tk), :]
        v = v_ref[pl.ds(off, tk), :]
        s = lax.dot_general(q, k, (((1,), (1,)), ((), ())), preferred_element_type=F32)
        m_new = jnp.maximum(m, jnp.max(s, axis=-1, keepdims=True))
        p = jnp.exp2(s - m_new)
        alpha = jnp.exp2(m - m_new)
        l = alpha * l + jnp.sum(p, axis=-1, keepdims=True)
        acc = alpha * acc + jnp.dot(p.astype(BF16), v, preferred_element_type=F32)
        return m_new, l, acc

    rows = group * tq
    init = (jnp.full((rows, 1), NEG_INF, F32), jnp.zeros((rows, 1), F32),
            jnp.zeros((rows, LANES), F32))
    _, l, acc = lax.fori_loop(0, nk, body, init, unroll=unroll)
    o = acc / l
    for g in range(group):
        sl = slice(g * LANES, (g + 1) * LANES)
        o_ref[:, sl] = (o[g * tq:(g + 1) * tq, :] * g_ref[:, sl]).astype(o_ref.dtype)


def _attn_a(z, gate, cfg, q_col0, k_col0, v_col0):
    s_len = z.shape[0]
    group = cfg.a_heads // cfg.a_kv_heads
    gw = group * LANES
    tq = min(cfg.tq_a, s_len)
    tk = min(cfg.tk_a, s_len)
    return pl.pallas_call(
        functools.partial(_attn_a_kernel, tk=tk, group=group, unroll=cfg.unroll_a),
        grid=(cfg.a_kv_heads, s_len // tq),
        scratch_shapes=[pltpu.VMEM((group * tq, LANES), BF16)],
        in_specs=[pl.BlockSpec((tq, gw), lambda kv, i: (i, q_col0 // gw + kv)),
                  pl.BlockSpec((s_len, LANES), lambda kv, i: (0, k_col0 // LANES + kv)),
                  pl.BlockSpec((s_len, LANES), lambda kv, i: (0, v_col0 // LANES + kv)),
                  pl.BlockSpec((tq, gw), lambda kv, i: (i, kv))],
        out_specs=pl.BlockSpec((tq, gw), lambda kv, i: (i, kv)),
        out_shape=jax.ShapeDtypeStruct((s_len, cfg.a_heads * LANES), BF16),
        compiler_params=_params("parallel", "arbitrary"),
        name="attn_global",
    )(z, z, z, gate)


NA_BLOCK_TYPES = 5


def _natten_row_geometry(rows):
    r0 = np.array([NA_ROWS // 2, 0, 2, rows - 4, rows - 2])
    u0 = np.clip(r0 - NA_ROWS // 2, 0, rows - NA_UNION_ROWS)
    r = r0[:, None] + np.arange(2)[None, :]
    rs = np.clip(r - NA_ROWS // 2, 0, rows - NA_ROWS)
    kr = u0[:, None] + np.arange(NA_UNION_ROWS)[None, :]
    vr = (kr[:, None, :] >= rs[:, :, None]) & (kr[:, None, :] < rs[:, :, None] + NA_ROWS)
    dr = kr[:, None, :] - r[:, :, None] + (NA_ROWS - 1)
    return dr, vr


def _build_natten_tables(rpb_ref, tbl_ref, rows):
    dr, vr = _natten_row_geometry(rows)
    c = lax.broadcasted_iota(jnp.int32, (GRID_W, LANES), 0)
    lane = lax.broadcasted_iota(jnp.int32, (GRID_W, LANES), 1)
    first = lane < GRID_W
    kc = jnp.where(first, lane, lane - GRID_W)
    cs = jnp.clip(c - NA_COLS // 2, 0, GRID_W - NA_COLS)
    vc = (kc >= cs) & (kc < cs + NA_COLS)
    base_shift = LANES - (NA_COLS - 1)
    for t in range(NA_BLOCK_TYPES):
        for i in range(2):
            for kp in range(NA_UNION_ROWS // 2):
                halves = []
                for half in range(2):
                    ku = 2 * kp + half
                    if vr[t, i, ku]:
                        row = jnp.broadcast_to(rpb_ref[int(dr[t, i, ku]):int(dr[t, i, ku]) + 1, :],
                                               (GRID_W, LANES))
                        halves.append(pltpu.roll(row, (base_shift + half * GRID_W) % LANES, 1,
                                                 stride=1, stride_axis=0))
                    else:
                        halves.append(None)
                lo, hi = halves
                if lo is None and hi is None:
                    tile = jnp.full((GRID_W, LANES), NEG_INF, F32)
                else:
                    ok = vc
                    if lo is None:
                        val, ok = hi, vc & ~first
                    elif hi is None:
                        val, ok = lo, vc & first
                    else:
                        val = jnp.where(first, lo, hi)
                    tile = jnp.where(ok, val * LOG2E, NEG_INF)
                tbl_ref[t, i * GRID_W:(i + 1) * GRID_W, kp * LANES:(kp + 1) * LANES] = tile


def _natten_kernel(q_ref, k_ref, v_ref, g_ref, rpb_ref, o_ref, tbl_ref, *, rows, unroll):
    qb = 2 * GRID_W
    span = NA_UNION_ROWS * GRID_W
    nb_step = q_ref.shape[0] // qb
    nb = rows // 2
    step = pl.program_id(1)

    @pl.when(step == 0)
    def _():
        _build_natten_tables(rpb_ref, tbl_ref, rows)

    def body(b, carry):
        blk = step * nb_step + b
        u0 = jnp.clip(2 * blk - NA_ROWS // 2, 0, rows - NA_UNION_ROWS)
        typ = jnp.where(blk == 0, 1, jnp.where(blk == 1, 2,
              jnp.where(blk == nb - 2, 3, jnp.where(blk == nb - 1, 4, 0))))
        qoff = pl.multiple_of(b * qb, qb)
        koff = pl.multiple_of(u0 * GRID_W, GRID_W)
        q = q_ref[pl.ds(qoff, qb), :]
        k = k_ref[pl.ds(koff, span), :]
        v = v_ref[pl.ds(koff, span), :]
        s = lax.dot_general(q, k, (((1,), (1,)), ((), ())), preferred_element_type=F32)
        s = s + tbl_ref[typ]
        m = jnp.max(s, axis=-1, keepdims=True)
        e = jnp.exp2(s - m)
        l = jnp.sum(e, axis=-1, keepdims=True)
        o = jnp.dot(e.astype(BF16), v, preferred_element_type=F32)
        o_ref[pl.ds(qoff, qb), :] = (o / l * g_ref[pl.ds(qoff, qb), :]).astype(o_ref.dtype)
        return carry

    lax.fori_loop(0, nb_step, body, 0, unroll=unroll)


def _natten(z, gate, rpb, cfg, q_col0, k_col0, v_col0, gate_col0):
    s_len = z.shape[0]
    rows = s_len // GRID_W
    tq = min(cfg.tq_b, s_len)
    span = NA_UNION_ROWS * GRID_W
    nh, nr, nc = rpb.shape
    rpb_pad = jnp.pad(rpb.astype(F32), ((0, 0), (0, 2 * NA_ROWS - nr), (0, LANES - nc)))
    return pl.pallas_call(
        functools.partial(_natten_kernel, rows=rows, unroll=cfg.unroll_b),
        grid=(cfg.b_heads, s_len // tq),
        in_specs=[pl.BlockSpec((tq, LANES), lambda h, i: (i, q_col0 // LANES + h)),
                  pl.BlockSpec((s_len, LANES), lambda h, i: (0, k_col0 // LANES + h)),
                  pl.BlockSpec((s_len, LANES), lambda h, i: (0, v_col0 // LANES + h)),
                  pl.BlockSpec((tq, LANES), lambda h, i: (i, gate_col0 // LANES + h)),
                  pl.BlockSpec((None, 2 * NA_ROWS, LANES), lambda h, i: (h, 0, 0))],
        out_specs=pl.BlockSpec((tq, LANES), lambda h, i: (i, h)),
        out_shape=jax.ShapeDtypeStruct((s_len, cfg.b_heads * LANES), BF16),
        scratch_shapes=[pltpu.VMEM((NA_BLOCK_TYPES, 2 * GRID_W, span), F32)],
        compiler_params=_params("parallel", "arbitrary"),
        name="attn_neighbourhood",
    )(z, z, z, gate, rpb_pad)


def _attn_c_kernel(slope_ref, sink_ref, q_ref, k_ref, v_ref, g_ref, o_ref, *, group):
    tq = q_ref.shape[0]
    s_len = k_ref.shape[0]
    span = tq + 2 * C_WINDOW
    kv = pl.program_id(0)
    t0 = pl.program_id(1) * tq
    start = pl.multiple_of(jnp.clip(t0 - C_WINDOW, 0, s_len - span), C_WINDOW)
    k = k_ref[pl.ds(start, span), :]
    v = v_ref[pl.ds(start, span), :]
    rel = (lax.broadcasted_iota(jnp.int32, (tq, span), 0)
           - lax.broadcasted_iota(jnp.int32, (tq, span), 1) + (t0 - start))
    dist = jnp.abs(rel)
    valid = dist <= C_WINDOW
    distf = dist.astype(F32)
    for g in range(group):
        head = kv * group + g
        sl = slice(g * LANES, (g + 1) * LANES)
        s = lax.dot_general(q_ref[:, sl], k, (((1,), (1,)), ((), ())), preferred_element_type=F32)
        s = jnp.where(valid, s - (slope_ref[head] * LOG2E) * distf, NEG_INF)
        sink = sink_ref[head] * LOG2E
        m = jnp.maximum(jnp.max(s, axis=-1, keepdims=True), sink)
        e = jnp.exp2(s - m)
        l = jnp.sum(e, axis=-1, keepdims=True) + jnp.exp2(sink - m)
        o = jnp.dot(e.astype(BF16), v, preferred_element_type=F32)
        o_ref[:, sl] = (o / l * g_ref[:, sl]).astype(o_ref.dtype)


def _attn_c(z, gate, slopes, sink, cfg, q_col0, k_col0, v_col0):
    s_len = z.shape[0]
    group = cfg.c_heads // cfg.c_kv_heads
    gw = group * LANES
    tq = cfg.tq_c
    assert s_len >= tq + 2 * C_WINDOW and s_len % tq == 0
    smem = pl.BlockSpec(memory_space=pltpu.SMEM)
    return pl.pallas_call(
        functools.partial(_attn_c_kernel, group=group),
        grid=(cfg.c_kv_heads, s_len // tq),
        in_specs=[smem, smem,
                  pl.BlockSpec((tq, gw), lambda kv, i: (i, q_col0 // gw + kv)),
                  pl.BlockSpec((s_len, LANES), lambda kv, i: (0, k_col0 // LANES + kv)),
                  pl.BlockSpec((s_len, LANES), lambda kv, i: (0, v_col0 // LANES + kv)),
                  pl.BlockSpec((tq, gw), lambda kv, i: (i, kv))],
        out_specs=pl.BlockSpec((tq, gw), lambda kv, i: (i, kv)),
        out_shape=jax.ShapeDtypeStruct((s_len, cfg.c_heads * LANES), BF16),
        compiler_params=_params("parallel", "arbitrary"),
        name="attn_window",
    )(slopes, sink, z, z, z, gate)


def _rope_tables(s_len):
    half = LANES // 2
    quarter = half // 2
    inv = jnp.exp(-math.log(ROPE_THETA) * jnp.arange(0, half, 2, dtype=F32) / half)
    pos = jnp.arange(s_len, dtype=jnp.int32)
    rows = (pos // GRID_W).astype(F32)
    cols = (pos % GRID_W).astype(F32)
    ang = jnp.concatenate([rows[:, None] * inv[None, :]] * 2 + [cols[:, None] * inv[None, :]] * 2, axis=1)
    first = (np.arange(LANES) % half) < quarter
    cos, sin = jnp.cos(ang), jnp.sin(ang)
    return cos, jnp.where(first[None], -sin, 0.0), jnp.where(first[None], 0.0, sin)


def _forward(x, norm_w, w_in_ab, w_out_ab, q_norm_a, k_norm_a, rpb_b, w_in_c, w_out_c, sink_c,
             final_norm_w, cfg):
    bsz, s_len, d = x.shape
    assert bsz == 1 and s_len % (2 * GRID_W) == 0 and s_len // GRID_W >= NA_UNION_ROWS
    depth = norm_w.shape[0]
    scale = LANES ** -0.5 * LOG2E
    a_q, a_kv, b_w = cfg.a_heads * LANES, cfg.a_kv_heads * LANES, cfg.b_heads * LANES
    c_q, c_kv = cfg.c_heads * LANES, cfg.c_kv_heads * LANES
    rope = _rope_tables(s_len)
    slopes = jnp.exp2(-8.0 * jnp.arange(1, cfg.c_heads + 1, dtype=F32) / cfg.c_heads)
    xs = x.reshape(s_len, d)
    for layer in range(depth):
        h = _rmsnorm(xs, norm_w[layer], BF16, cfg.t_norm)
        if layer % 2 == 0:
            e = layer // 2
            ka0, va0 = a_q, a_q + a_kv
            qb0 = va0 + a_kv
            kb0, vb0 = qb0 + b_w, qb0 + 2 * b_w
            g0 = vb0 + b_w
            regions = ((0, ka0, "q_rope", scale), (ka0, va0, "k_rope", 1.0), (va0, qb0, "cast", 1.0),
                       (qb0, kb0, "cast", scale), (kb0, g0, "cast", 1.0))
            z = _inproj(h, w_in_ab, e, regions, g0, cfg, rope + (q_norm_a[e], k_norm_a[e]))
            gate = _gate_proj(h, w_in_ab, e, g0, a_q + b_w, cfg)
            ya = _attn_a(z, gate, cfg, 0, ka0, va0)
            yb = _natten(z, gate, rpb_b[e], cfg, qb0, kb0, vb0, a_q)
            xs = _outproj([ya, yb], w_out_ab, e, xs, cfg)
        else:
            o = layer // 2
            kc0, vc0 = c_q, c_q + c_kv
            g0 = vc0 + c_kv
            regions = ((0, kc0, "cast", scale), (kc0, g0, "cast", 1.0))
            z = _inproj(h, w_in_c, o, regions, g0, cfg)
            gate = _gate_proj(h, w_in_c, o, g0, c_q, cfg)
            y = _attn_c(z, gate, slopes, sink_c[o].astype(F32), cfg, 0, kc0, vc0)
            xs = _outproj([y], w_out_c, o, xs, cfg)
    out = _rmsnorm(xs, final_norm_w, x.dtype, cfg.t_norm)
    return out.reshape(bsz, s_len, d)


def kernel(x, norm_w, w_in_ab, w_out_ab, q_norm_a, k_norm_a, rpb_b, w_in_c, w_out_c, sink_c, final_norm_w):
    return _forward(x, norm_w, w_in_ab, w_out_ab, q_norm_a, k_norm_a, rpb_b, w_in_c, w_out_c, sink_c,
                    final_norm_w, Config())
```

```python
import functools
import math
from typing import NamedTuple

import jax
import jax.numpy as jnp
import numpy as np
from jax import lax
from jax.experimental import pallas as pl
from jax.experimental.pallas import tpu as pltpu

F32 = jnp.float32
BF16 = jnp.bfloat16

LANES = 128
NORM_EPS = 1e-6
NEG_INF = -1e30
LOG2E = math.log2(math.e)
ROPE_THETA = 10000.0
GRID_W = 64
NA_ROWS = 8
NA_COLS = 16
NA_QB_ROWS = 4
NA_UNION_ROWS = 12
C_WINDOW = 128
VMEM_LIMIT_BYTES = 56 * 1024 * 1024


class Config(NamedTuple):
    a_heads: int = 16
    a_kv_heads: int = 4
    b_heads: int = 16
    c_heads: int = 32
    c_kv_heads: int = 8
    tm: int = 1024
    tn: int = 512
    tq_a: int = 256
    tk_a: int = 1024
    chunk_a: int = 16
    tq_c: int = 512
    sub_c: int = 128
    tq_b: int = 2048
    t_norm: int = 256


def _params(*sem):
    return pltpu.CompilerParams(dimension_semantics=sem, vmem_limit_bytes=VMEM_LIMIT_BYTES)


def _rmsnorm_kernel(x_ref, w_ref, o_ref):
    x = x_ref[...]
    ms = jnp.mean(x * x, axis=-1, keepdims=True)
    o_ref[...] = (x * lax.rsqrt(ms + NORM_EPS) * w_ref[...]).astype(o_ref.dtype)


def _rmsnorm(x, w, out_dtype, t):
    m, d = x.shape
    return pl.pallas_call(
        _rmsnorm_kernel,
        grid=(m // t,),
        in_specs=[pl.BlockSpec((t, d), lambda i: (i, 0)),
                  pl.BlockSpec((1, d), lambda i: (0, 0))],
        out_specs=pl.BlockSpec((t, d), lambda i: (i, 0)),
        out_shape=jax.ShapeDtypeStruct((m, d), out_dtype),
        compiler_params=_params("parallel"),
        name="rmsnorm",
    )(x, w.reshape(1, d))


def _stage_weight(w_ref, wbf_ref):
    @pl.when(pl.program_id(1) == 0)
    def _():
        wbf_ref[...] = w_ref[...].astype(BF16)


def _norm_rope(z, w, cos, sin_lo, sin_hi):
    ms = jnp.mean(z * z, axis=-1, keepdims=True)
    y = z * lax.rsqrt(ms + NORM_EPS) * w
    quarter = LANES // 4
    return (y * cos + pltpu.roll(y, LANES - quarter, 1) * sin_lo
            + pltpu.roll(y, quarter, 1) * sin_hi)


def _inproj_kernel(*refs, regions, rope):
    if rope:
        h_ref, w_ref, cos_ref, slo_ref, shi_ref, qn_ref, kn_ref, o_ref, wbf_ref = refs
    else:
        h_ref, w_ref, o_ref, wbf_ref = refs
    _stage_weight(w_ref, wbf_ref)
    j = pl.program_id(0)
    acc = jnp.dot(h_ref[...], wbf_ref[...], preferred_element_type=F32)
    tn = acc.shape[1]
    for lo, hi, kind, scale in regions:
        @pl.when((j >= lo) & (j < hi))
        def _(kind=kind, scale=scale):
            if kind == "cast":
                o_ref[...] = (acc * scale if scale != 1.0 else acc).astype(o_ref.dtype)
            else:
                nw = (qn_ref if kind == "q_rope" else kn_ref)[...]
                cos, slo, shi = cos_ref[...], slo_ref[...], shi_ref[...]
                for hh in range(tn // LANES):
                    sl = slice(hh * LANES, (hh + 1) * LANES)
                    y = _norm_rope(acc[:, sl], nw, cos, slo, shi)
                    o_ref[:, sl] = (y * scale if scale != 1.0 else y).astype(o_ref.dtype)


def _inproj(h, w, e, regions, n_out, cfg, rope_args=None):
    m, k = h.shape
    tm, tn = min(cfg.tm, m), cfg.tn
    row = lambda j, i: (i, 0)
    in_specs = [pl.BlockSpec((tm, k), row),
                pl.BlockSpec((None, k, tn), lambda j, i: (e, 0, j))]
    args = [h, w]
    if rope_args is not None:
        cos, slo, shi, qn, kn = rope_args
        in_specs += [pl.BlockSpec((tm, LANES), row)] * 3
        in_specs += [pl.BlockSpec((1, LANES), lambda j, i: (0, 0))] * 2
        args += [cos, slo, shi, qn.reshape(1, LANES), kn.reshape(1, LANES)]
    tiles = tuple((lo // tn, hi // tn, kind, scale) for lo, hi, kind, scale in regions)
    return pl.pallas_call(
        functools.partial(_inproj_kernel, regions=tiles, rope=rope_args is not None),
        grid=(n_out // tn, m // tm),
        in_specs=in_specs,
        out_specs=pl.BlockSpec((tm, tn), lambda j, i: (i, j)),
        out_shape=jax.ShapeDtypeStruct((m, n_out), BF16),
        scratch_shapes=[pltpu.VMEM((k, tn), BF16)],
        compiler_params=_params("parallel", "arbitrary"),
        name="inproj",
    )(*args)


def _gate_kernel(h_ref, w_ref, o_ref, wbf_ref):
    _stage_weight(w_ref, wbf_ref)
    z = jnp.dot(h_ref[...], wbf_ref[...], preferred_element_type=F32)
    o_ref[...] = z * (0.5 * jnp.tanh(0.5 * z) + 0.5)


def _gate_proj(h, w, e, col0, n_out, cfg):
    m, k = h.shape
    tm, tn = min(cfg.tm, m), cfg.tn
    off = col0 // tn
    return pl.pallas_call(
        _gate_kernel,
        grid=(n_out // tn, m // tm),
        in_specs=[pl.BlockSpec((tm, k), lambda j, i: (i, 0)),
                  pl.BlockSpec((None, k, tn), lambda j, i: (e, 0, j + off))],
        out_specs=pl.BlockSpec((tm, tn), lambda j, i: (i, j)),
        out_shape=jax.ShapeDtypeStruct((m, n_out), F32),
        scratch_shapes=[pltpu.VMEM((k, tn), BF16)],
        compiler_params=_params("parallel", "arbitrary"),
        name="gateproj",
    )(h, w)


def _outproj_kernel(*refs, n_lhs):
    y_refs = refs[:n_lhs]
    w_ref, x_ref, o_ref, wbf_ref = refs[n_lhs:]
    _stage_weight(w_ref, wbf_ref)
    acc = x_ref[...]
    k0 = 0
    for y_ref in y_refs:
        kk = y_ref.shape[1]
        acc = acc + jnp.dot(y_ref[...], wbf_ref[k0:k0 + kk, :], preferred_element_type=F32)
        k0 += kk
    o_ref[...] = acc


def _outproj(ys, w, e, x, cfg):
    m, n = x.shape
    k = w.shape[1]
    tm, tn = min(cfg.tm, m), cfg.tn
    in_specs = [pl.BlockSpec((tm, y.shape[1]), lambda j, i: (i, 0)) for y in ys]
    in_specs += [pl.BlockSpec((None, k, tn), lambda j, i: (e, 0, j)),
                 pl.BlockSpec((tm, tn), lambda j, i: (i, j))]
    return pl.pallas_call(
        functools.partial(_outproj_kernel, n_lhs=len(ys)),
        grid=(n // tn, m // tm),
        in_specs=in_specs,
        out_specs=pl.BlockSpec((tm, tn), lambda j, i: (i, j)),
        out_shape=jax.ShapeDtypeStruct((m, n), F32),
        scratch_shapes=[pltpu.VMEM((k, tn), BF16)],
        compiler_params=_params("parallel", "arbitrary"),
        name="outproj",
    )(*ys, w, x)


def _attn_a_kernel(q_ref, k_ref, v_ref, g_ref, o_ref, qs_ref, v1_ref, s_ref, p_ref, m_ref, alpha_ref,
                   acc_ref, *, tk, group, chunk):
    tq = q_ref.shape[0]
    s_len = k_ref.shape[0]
    nk = s_len // tk

    @pl.when(pl.program_id(1) == 0)
    def _():
        v1_ref[:, :LANES] = v_ref[...]
        v1_ref[:, LANES:] = jnp.ones((s_len, LANES), BF16)

    for g in range(group):
        qs_ref[g * tq:(g + 1) * tq, :] = q_ref[:, g * LANES:(g + 1) * LANES]
    rows = group * tq
    m_ref[...] = jnp.full((rows, 1), NEG_INF, F32)
    acc_ref[...] = jnp.zeros((rows, 2 * LANES), F32)

    def scores(t):
        k = k_ref[t * tk:(t + 1) * tk, :]
        s_ref[t % 2] = lax.dot_general(qs_ref[...], k, (((1,), (1,)), ((), ())),
                                       preferred_element_type=F32)

    scores(0)
    for t in range(nk):
        if t + 1 < nk:
            scores(t + 1)
        for c in range(rows // chunk):
            rs = slice(c * chunk, (c + 1) * chunk)
            s = s_ref[t % 2, rs, :]
            m_old = m_ref[rs, :]
            m_new = jnp.maximum(m_old, jnp.max(s, axis=-1, keepdims=True))
            m_ref[rs, :] = m_new
            alpha_ref[rs, :] = jnp.exp2(m_old - m_new)
            p_ref[t % 2, rs, :] = jnp.exp2(s - m_new).astype(BF16)
        pv = jnp.dot(p_ref[t % 2], v1_ref[t * tk:(t + 1) * tk, :], preferred_element_type=F32)
        acc_ref[...] = alpha_ref[...] * acc_ref[...] + pv
    o = acc_ref[:, :LANES] / acc_ref[:, LANES:]
    for g in range(group):
        sl = slice(g * LANES, (g + 1) * LANES)
        o_ref[:, sl] = (o[g * tq:(g + 1) * tq, :] * g_ref[:, sl]).astype(o_ref.dtype)


def _attn_a(z, gate, cfg, q_col0, k_col0, v_col0):
    s_len = z.shape[0]
    group = cfg.a_heads // cfg.a_kv_heads
    gw = group * LANES
    tq = min(cfg.tq_a, s_len)
    tk = min(cfg.tk_a, s_len)
    return pl.pallas_call(
        functools.partial(_attn_a_kernel, tk=tk, group=group, chunk=cfg.chunk_a),
        grid=(cfg.a_kv_heads, s_len // tq),
        scratch_shapes=[pltpu.VMEM((group * tq, LANES), BF16),
                        pltpu.VMEM((s_len, 2 * LANES), BF16),
                        pltpu.VMEM((2, group * tq, tk), F32),
                        pltpu.VMEM((2, group * tq, tk), BF16),
                        pltpu.VMEM((group * tq, 1), F32),
                        pltpu.VMEM((group * tq, 1), F32),
                        pltpu.VMEM((group * tq, 2 * LANES), F32)],
        in_specs=[pl.BlockSpec((tq, gw), lambda kv, i: (i, q_col0 // gw + kv)),
                  pl.BlockSpec((s_len, LANES), lambda kv, i: (0, k_col0 // LANES + kv)),
                  pl.BlockSpec((s_len, LANES), lambda kv, i: (0, v_col0 // LANES + kv)),
                  pl.BlockSpec((tq, gw), lambda kv, i: (i, kv))],
        out_specs=pl.BlockSpec((tq, gw), lambda kv, i: (i, kv)),
        out_shape=jax.ShapeDtypeStruct((s_len, cfg.a_heads * LANES), BF16),
        compiler_params=_params("parallel", "arbitrary"),
        name="attn_global",
    )(z, z, z, gate)


def _natten_geometry(rows):
    types, block_type = [], []
    for b in range(rows // NA_QB_ROWS):
        r = b * NA_QB_ROWS + np.arange(NA_QB_ROWS)
        u0 = int(np.clip(r[0] - NA_ROWS // 2, 0, rows - NA_UNION_ROWS))
        rs = np.clip(r - NA_ROWS // 2, 0, rows - NA_ROWS)
        assert u0 <= rs.min() and rs.max() + NA_ROWS <= u0 + NA_UNION_ROWS
        kr = u0 + np.arange(NA_UNION_ROWS)
        vr = (kr[None, :] >= rs[:, None]) & (kr[None, :] < rs[:, None] + NA_ROWS)
        dr = np.where(vr, kr[None, :] - r[:, None] + (NA_ROWS - 1), 0)
        for t, (dr_t, vr_t) in enumerate(types):
            if np.array_equal(dr, dr_t) and np.array_equal(vr, vr_t):
                block_type.append(t)
                break
        else:
            block_type.append(len(types))
            types.append((dr, vr))
    return types, block_type


def _build_natten_tables(rpb_ref, tbl_ref, types):
    c = lax.broadcasted_iota(jnp.int32, (GRID_W, LANES), 0)
    lane = lax.broadcasted_iota(jnp.int32, (GRID_W, LANES), 1)
    first = lane < GRID_W
    kc = jnp.where(first, lane, lane - GRID_W)
    cs = jnp.clip(c - NA_COLS // 2, 0, GRID_W - NA_COLS)
    vc = (kc >= cs) & (kc < cs + NA_COLS)
    base_shift = LANES - (NA_COLS - 1)
    for t, (dr, vr) in enumerate(types):
        for i in range(NA_QB_ROWS):
            for kp in range(NA_UNION_ROWS // 2):
                halves = []
                for half in range(2):
                    ku = 2 * kp + half
                    if vr[i, ku]:
                        row = jnp.broadcast_to(rpb_ref[int(dr[i, ku]):int(dr[i, ku]) + 1, :],
                                               (GRID_W, LANES))
                        halves.append(pltpu.roll(row, (base_shift + half * GRID_W) % LANES, 1,
                                                 stride=1, stride_axis=0))
                    else:
                        halves.append(None)
                lo, hi = halves
                if lo is None and hi is None:
                    tile = jnp.full((GRID_W, LANES), NEG_INF, F32)
                else:
                    ok = vc
                    if lo is None:
                        val, ok = hi, vc & ~first
                    elif hi is None:
                        val, ok = lo, vc & first
                    else:
                        val = jnp.where(first, lo, hi)
                    tile = jnp.where(ok, val * LOG2E, NEG_INF)
                tbl_ref[t, i * GRID_W:(i + 1) * GRID_W, kp * LANES:(kp + 1) * LANES] = tile


def _natten_kernel(q_ref, k_ref, v_ref, g_ref, rpb_ref, o_ref, tbl_ref, v1_ref, *, rows):
    qb = NA_QB_ROWS * GRID_W
    span = NA_UNION_ROWS * GRID_W
    nb_step = q_ref.shape[0] // qb
    types, block_type = _natten_geometry(rows)
    common = max(set(block_type), key=block_type.count)
    step = pl.program_id(1)

    @pl.when(step == 0)
    def _():
        _build_natten_tables(rpb_ref, tbl_ref, types)
        v1_ref[:, :LANES] = v_ref[...]
        v1_ref[:, LANES:] = jnp.ones((v_ref.shape[0], LANES), BF16)

    for b in range(nb_step):
        blk = step * nb_step + b
        u0 = jnp.clip(blk * NA_QB_ROWS - NA_ROWS // 2, 0, rows - NA_UNION_ROWS)
        typ = common
        for bb, t in enumerate(block_type):
            if t != common:
                typ = jnp.where(blk == bb, t, typ)
        koff = pl.multiple_of(u0 * GRID_W, GRID_W)
        qs = slice(b * qb, (b + 1) * qb)
        k = k_ref[pl.ds(koff, span), :]
        v1 = v1_ref[pl.ds(koff, span), :]
        s = lax.dot_general(q_ref[qs, :], k, (((1,), (1,)), ((), ())), preferred_element_type=F32)
        s = s + tbl_ref[typ]
        m = jnp.max(s, axis=-1, keepdims=True)
        pv = jnp.dot(jnp.exp2(s - m).astype(BF16), v1, preferred_element_type=F32)
        o_ref[qs, :] = (pv[:, :LANES] / pv[:, LANES:] * g_ref[qs, :]).astype(o_ref.dtype)


def _natten(z, gate, rpb, cfg, q_col0, k_col0, v_col0, gate_col0):
    s_len = z.shape[0]
    rows = s_len // GRID_W
    tq = min(cfg.tq_b, s_len)
    qb = NA_QB_ROWS * GRID_W
    span = NA_UNION_ROWS * GRID_W
    assert rows % NA_QB_ROWS == 0 and rows >= NA_UNION_ROWS and tq % qb == 0
    n_types = len(_natten_geometry(rows)[0])
    nh, nr, nc = rpb.shape
    rpb_pad = jnp.pad(rpb.astype(F32), ((0, 0), (0, 2 * NA_ROWS - nr), (0, LANES - nc)))
    return pl.pallas_call(
        functools.partial(_natten_kernel, rows=rows),
        grid=(cfg.b_heads, s_len // tq),
        in_specs=[pl.BlockSpec((tq, LANES), lambda h, i: (i, q_col0 // LANES + h)),
                  pl.BlockSpec((s_len, LANES), lambda h, i: (0, k_col0 // LANES + h)),
                  pl.BlockSpec((s_len, LANES), lambda h, i: (0, v_col0 // LANES + h)),
                  pl.BlockSpec((tq, LANES), lambda h, i: (i, gate_col0 // LANES + h)),
                  pl.BlockSpec((None, 2 * NA_ROWS, LANES), lambda h, i: (h, 0, 0))],
        out_specs=pl.BlockSpec((tq, LANES), lambda h, i: (i, h)),
        out_shape=jax.ShapeDtypeStruct((s_len, cfg.b_heads * LANES), BF16),
        scratch_shapes=[pltpu.VMEM((n_types, qb, span), F32),
                        pltpu.VMEM((s_len, 2 * LANES), BF16)],
        compiler_params=_params("parallel", "arbitrary"),
        name="attn_neighbourhood",
    )(z, z, z, gate, rpb_pad)


def _attn_c_kernel(slope_ref, sink_ref, q_ref, k_ref, v_ref, g_ref, o_ref, v1_ref, *, group, sub):
    tq = q_ref.shape[0]
    s_len = k_ref.shape[0]
    span = sub + 2 * C_WINDOW
    kv = pl.program_id(0)

    @pl.when(pl.program_id(1) == 0)
    def _():
        v1_ref[:, :LANES] = v_ref[...]
        v1_ref[:, LANES:] = jnp.ones((s_len, LANES), BF16)

    slopes = [slope_ref[kv * group + g] * LOG2E for g in range(group)]
    sinks = [sink_ref[kv * group + g] * LOG2E for g in range(group)]
    row = lax.broadcasted_iota(jnp.int32, (sub, span), 0)
    col = lax.broadcasted_iota(jnp.int32, (sub, span), 1)
    for b in range(tq // sub):
        t0 = pl.program_id(1) * tq + b * sub
        start = pl.multiple_of(jnp.clip(t0 - C_WINDOW, 0, s_len - span), C_WINDOW)
        k = k_ref[pl.ds(start, span), :]
        v1 = v1_ref[pl.ds(start, span), :]
        rows = slice(b * sub, (b + 1) * sub)
        q = jnp.concatenate([q_ref[rows, g * LANES:(g + 1) * LANES] for g in range(group)], axis=0)
        s = lax.dot_general(q, k, (((1,), (1,)), ((), ())), preferred_element_type=F32)
        dist = jnp.abs(row - col + (t0 - start))
        pen = jnp.where(dist <= C_WINDOW, dist.astype(F32), -NEG_INF)
        ps, ms = [], []
        for g in range(group):
            sg = s[g * sub:(g + 1) * sub, :] - slopes[g] * pen
            m = jnp.maximum(jnp.max(sg, axis=-1, keepdims=True), sinks[g])
            ps.append(jnp.exp2(sg - m).astype(BF16))
            ms.append(m)
        pv = jnp.dot(jnp.concatenate(ps, axis=0), v1, preferred_element_type=F32)
        for g in range(group):
            sl = slice(g * LANES, (g + 1) * LANES)
            pg = pv[g * sub:(g + 1) * sub, :]
            l = pg[:, LANES:] + jnp.exp2(sinks[g] - ms[g])
            o_ref[rows, sl] = (pg[:, :LANES] / l * g_ref[rows, sl]).astype(o_ref.dtype)


def _attn_c(z, gate, slopes, sink, cfg, q_col0, k_col0, v_col0):
    s_len = z.shape[0]
    group = cfg.c_heads // cfg.c_kv_heads
    gw = group * LANES
    tq, sub = cfg.tq_c, cfg.sub_c
    assert s_len >= sub + 2 * C_WINDOW and s_len % tq == 0 and tq % sub == 0 and sub % C_WINDOW == 0
    smem = pl.BlockSpec(memory_space=pltpu.SMEM)
    return pl.pallas_call(
        functools.partial(_attn_c_kernel, group=group, sub=sub),
        grid=(cfg.c_kv_heads, s_len // tq),
        scratch_shapes=[pltpu.VMEM((s_len, 2 * LANES), BF16)],
        in_specs=[smem, smem,
                  pl.BlockSpec((tq, gw), lambda kv, i: (i, q_col0 // gw + kv)),
                  pl.BlockSpec((s_len, LANES), lambda kv, i: (0, k_col0 // LANES + kv)),
                  pl.BlockSpec((s_len, LANES), lambda kv, i: (0, v_col0 // LANES + kv)),
                  pl.BlockSpec((tq, gw), lambda kv, i: (i, kv))],
        out_specs=pl.BlockSpec((tq, gw), lambda kv, i: (i, kv)),
        out_shape=jax.ShapeDtypeStruct((s_len, cfg.c_heads * LANES), BF16),
        compiler_params=_params("parallel", "arbitrary"),
        name="attn_window",
    )(slopes, sink, z, z, z, gate)


def _rope_tables(s_len):
    half = LANES // 2
    quarter = half // 2
    inv = jnp.exp(-math.log(ROPE_THETA) * jnp.arange(0, half, 2, dtype=F32) / half)
    pos = jnp.arange(s_len, dtype=jnp.int32)
    rows = (pos // GRID_W).astype(F32)
    cols = (pos % GRID_W).astype(F32)
    ang = jnp.concatenate([rows[:, None] * inv[None, :]] * 2 + [cols[:, None] * inv[None, :]] * 2, axis=1)
    first = (np.arange(LANES) % half) < quarter
    cos, sin = jnp.cos(ang), jnp.sin(ang)
    return cos, jnp.where(first[None], -sin, 0.0), jnp.where(first[None], 0.0, sin)


def _forward(x, norm_w, w_in_ab, w_out_ab, q_norm_a, k_norm_a, rpb_b, w_in_c, w_out_c, sink_c,
             final_norm_w, cfg):
    bsz, s_len, d = x.shape
    assert bsz == 1 and s_len % GRID_W == 0
    depth = norm_w.shape[0]
    scale = LANES ** -0.5 * LOG2E
    a_q, a_kv, b_w = cfg.a_heads * LANES, cfg.a_kv_heads * LANES, cfg.b_heads * LANES
    c_q, c_kv = cfg.c_heads * LANES, cfg.c_kv_heads * LANES
    rope = _rope_tables(s_len)
    slopes = jnp.exp2(-8.0 * jnp.arange(1, cfg.c_heads + 1, dtype=F32) / cfg.c_heads)
    xs = x.reshape(s_len, d)
    for layer in range(depth):
        h = _rmsnorm(xs, norm_w[layer], BF16, cfg.t_norm)
        if layer % 2 == 0:
            e = layer // 2
            ka0, va0 = a_q, a_q + a_kv
            qb0 = va0 + a_kv
            kb0, vb0 = qb0 + b_w, qb0 + 2 * b_w
            g0 = vb0 + b_w
            regions = ((0, ka0, "q_rope", scale), (ka0, va0, "k_rope", 1.0), (va0, qb0, "cast", 1.0),
                       (qb0, kb0, "cast", scale), (kb0, g0, "cast", 1.0))
            z = _inproj(h, w_in_ab, e, regions, g0, cfg, rope + (q_norm_a[e], k_norm_a[e]))
            gate = _gate_proj(h, w_in_ab, e, g0, a_q + b_w, cfg)
            ya = _attn_a(z, gate, cfg, 0, ka0, va0)
            yb = _natten(z, gate, rpb_b[e], cfg, qb0, kb0, vb0, a_q)
            xs = _outproj([ya, yb], w_out_ab, e, xs, cfg)
        else:
            o = layer // 2
            kc0, vc0 = c_q, c_q + c_kv
            g0 = vc0 + c_kv
            regions = ((0, kc0, "cast", scale), (kc0, g0, "cast", 1.0))
            z = _inproj(h, w_in_c, o, regions, g0, cfg)
            gate = _gate_proj(h, w_in_c, o, g0, c_q, cfg)
            y = _attn_c(z, gate, slopes, sink_c[o].astype(F32), cfg, 0, kc0, vc0)
            xs = _outproj([y], w_out_c, o, xs, cfg)
    out = _rmsnorm(xs, final_norm_w, x.dtype, cfg.t_norm)
    return out.reshape(bsz, s_len, d)


def kernel(x, norm_w, w_in_ab, w_out_ab, q_norm_a, k_norm_a, rpb_b, w_in_c, w_out_c, sink_c, final_norm_w):
    return _forward(x, norm_w, w_in_ab, w_out_ab, q_norm_a, k_norm_a, rpb_b, w_in_c, w_out_c, sink_c,
                    final_norm_w, Config())
```

```python
import functools
import math
from typing import NamedTuple

import jax
import jax.numpy as jnp
import numpy as np
from jax import lax
from jax.experimental import pallas as pl
from jax.experimental.pallas import tpu as pltpu

F32 = jnp.float32
BF16 = jnp.bfloat16

LANES = 128
NORM_EPS = 1e-6
NEG_INF = -1e30
LOG2E = math.log2(math.e)
ROPE_THETA = 10000.0
GRID_W = 64
NA_ROWS = 8
NA_COLS = 16
NA_QB_ROWS = 4
NA_UNION_ROWS = 12
C_WINDOW = 128
PROJ_SPLIT = 4
A_SAFE_SHIFT = 60.0
VMEM_LIMIT_BYTES = 56 * 1024 * 1024


class Config(NamedTuple):
    a_heads: int = 16
    a_kv_heads: int = 4
    b_heads: int = 16
    c_heads: int = 32
    c_kv_heads: int = 8
    tm: int = 1024
    tn: int = 512
    tq_a: int = 256
    tk_a: int = 1024
    chunk_a: int = 16
    tq_c: int = 512
    sub_c: int = 128
    tq_b: int = 2048
    t_norm: int = 256


def _params(*sem):
    return pltpu.CompilerParams(dimension_semantics=sem, vmem_limit_bytes=VMEM_LIMIT_BYTES)


def _rmsnorm_kernel(x_ref, w_ref, o_ref):
    x = x_ref[...]
    ms = jnp.mean(x * x, axis=-1, keepdims=True)
    o_ref[...] = (x * lax.rsqrt(ms + NORM_EPS) * w_ref[...]).astype(o_ref.dtype)


def _rmsnorm(x, w, out_dtype, t):
    m, d = x.shape
    return pl.pallas_call(
        _rmsnorm_kernel,
        grid=(m // t,),
        in_specs=[pl.BlockSpec((t, d), lambda i: (i, 0)),
                  pl.BlockSpec((1, d), lambda i: (0, 0))],
        out_specs=pl.BlockSpec((t, d), lambda i: (i, 0)),
        out_shape=jax.ShapeDtypeStruct((m, d), out_dtype),
        compiler_params=_params("parallel"),
        name="rmsnorm",
    )(x, w.reshape(1, d))


def _stage_weight(w_ref, wbf_ref):
    @pl.when(pl.program_id(1) == 0)
    def _():
        wbf_ref[...] = w_ref[...].astype(BF16)


def _norm_rope(z, w, cos, sin_lo, sin_hi):
    ms = jnp.mean(z * z, axis=-1, keepdims=True)
    y = z * lax.rsqrt(ms + NORM_EPS) * w
    quarter = LANES // 4
    return (y * cos + pltpu.roll(y, LANES - quarter, 1) * sin_lo
            + pltpu.roll(y, quarter, 1) * sin_hi)


def _row_blocks(tm):
    step = tm // PROJ_SPLIT
    return [slice(r * step, (r + 1) * step) for r in range(PROJ_SPLIT)]


def _rope_proj_kernel(h_ref, w_ref, cos_ref, slo_ref, shi_ref, qn_ref, kn_ref, o_ref, wbf_ref, *,
                      q_tiles, q_scale):
    _stage_weight(w_ref, wbf_ref)
    nw = jnp.where(pl.program_id(0) < q_tiles, qn_ref[...] * q_scale, kn_ref[...])
    for rs in _row_blocks(h_ref.shape[0]):
        acc = jnp.dot(h_ref[rs, :], wbf_ref[...], preferred_element_type=F32)
        cos, slo, shi = cos_ref[rs, :], slo_ref[rs, :], shi_ref[rs, :]
        for hh in range(acc.shape[1] // LANES):
            sl = slice(hh * LANES, (hh + 1) * LANES)
            o_ref[rs, sl] = _norm_rope(acc[:, sl], nw, cos, slo, shi).astype(o_ref.dtype)


def _rope_proj(h, w, e, n_q, n_k, q_scale, rope, qn, kn, cfg):
    m, k = h.shape
    tm, tn = min(cfg.tm, m), cfg.tn
    row = lambda j, i: (i, 0)
    vec = pl.BlockSpec((1, LANES), lambda j, i: (0, 0))
    return pl.pallas_call(
        functools.partial(_rope_proj_kernel, q_tiles=n_q // tn, q_scale=q_scale),
        grid=((n_q + n_k) // tn, m // tm),
        in_specs=[pl.BlockSpec((tm, k), row),
                  pl.BlockSpec((None, k, tn), lambda j, i: (e, 0, j)),
                  pl.BlockSpec((tm, LANES), row), pl.BlockSpec((tm, LANES), row),
                  pl.BlockSpec((tm, LANES), row), vec, vec],
        out_specs=pl.BlockSpec((tm, tn), lambda j, i: (i, j)),
        out_shape=jax.ShapeDtypeStruct((m, n_q + n_k), BF16),
        scratch_shapes=[pltpu.VMEM((k, tn), BF16)],
        compiler_params=_params("parallel", "arbitrary"),
        name="ropeproj",
    )(h, w, *rope, qn.reshape(1, LANES), kn.reshape(1, LANES))


def _cast_proj_kernel(scale_ref, h_ref, w_ref, o_ref, wbf_ref):
    _stage_weight(w_ref, wbf_ref)
    scale = scale_ref[pl.program_id(0)]
    for rs in _row_blocks(h_ref.shape[0]):
        acc = jnp.dot(h_ref[rs, :], wbf_ref[...], preferred_element_type=F32)
        o_ref[rs, :] = (acc * scale).astype(o_ref.dtype)


def _cast_proj(h, w, e, col0, widths_scales, cfg):
    m, k = h.shape
    tm, tn = min(cfg.tm, m), cfg.tn
    scales = np.concatenate([np.full(width // tn, scale, np.float32) for width, scale in widths_scales])
    n_out = tn * len(scales)
    off = col0 // tn
    return pl.pallas_call(
        _cast_proj_kernel,
        grid=(n_out // tn, m // tm),
        in_specs=[pl.BlockSpec(memory_space=pltpu.SMEM),
                  pl.BlockSpec((tm, k), lambda j, i: (i, 0)),
                  pl.BlockSpec((None, k, tn), lambda j, i: (e, 0, j + off))],
        out_specs=pl.BlockSpec((tm, tn), lambda j, i: (i, j)),
        out_shape=jax.ShapeDtypeStruct((m, n_out), BF16),
        scratch_shapes=[pltpu.VMEM((k, tn), BF16)],
        compiler_params=_params("parallel", "arbitrary"),
        name="castproj",
    )(jnp.asarray(scales), h, w)


def _gate_kernel(h_ref, w_ref, o_ref, wbf_ref):
    _stage_weight(w_ref, wbf_ref)
    for rs in _row_blocks(h_ref.shape[0]):
        z = jnp.dot(h_ref[rs, :], wbf_ref[...], preferred_element_type=F32)
        o_ref[rs, :] = z * (0.5 * jnp.tanh(0.5 * z) + 0.5)


def _gate_proj(h, w, e, col0, n_out, cfg):
    m, k = h.shape
    tm, tn = min(cfg.tm, m), cfg.tn
    off = col0 // tn
    return pl.pallas_call(
        _gate_kernel,
        grid=(n_out // tn, m // tm),
        in_specs=[pl.BlockSpec((tm, k), lambda j, i: (i, 0)),
                  pl.BlockSpec((None, k, tn), lambda j, i: (e, 0, j + off))],
        out_specs=pl.BlockSpec((tm, tn), lambda j, i: (i, j)),
        out_shape=jax.ShapeDtypeStruct((m, n_out), F32),
        scratch_shapes=[pltpu.VMEM((k, tn), BF16)],
        compiler_params=_params("parallel", "arbitrary"),
        name="gateproj",
    )(h, w)


def _outproj_kernel(*refs, n_lhs):
    y_refs = refs[:n_lhs]
    w_ref, x_ref, o_ref, wbf_ref = refs[n_lhs:]
    _stage_weight(w_ref, wbf_ref)
    acc = x_ref[...]
    k0 = 0
    for y_ref in y_refs:
        kk = y_ref.shape[1]
        acc = acc + jnp.dot(y_ref[...], wbf_ref[k0:k0 + kk, :], preferred_element_type=F32)
        k0 += kk
    o_ref[...] = acc


def _outproj(ys, w, e, x, cfg):
    m, n = x.shape
    k = w.shape[1]
    tm, tn = min(cfg.tm, m), cfg.tn
    in_specs = [pl.BlockSpec((tm, y.shape[1]), lambda j, i: (i, 0)) for y in ys]
    in_specs += [pl.BlockSpec((None, k, tn), lambda j, i: (e, 0, j)),
                 pl.BlockSpec((tm, tn), lambda j, i: (i, j))]
    return pl.pallas_call(
        functools.partial(_outproj_kernel, n_lhs=len(ys)),
        grid=(n // tn, m // tm),
        in_specs=in_specs,
        out_specs=pl.BlockSpec((tm, tn), lambda j, i: (i, j)),
        out_shape=jax.ShapeDtypeStruct((m, n), F32),
        scratch_shapes=[pltpu.VMEM((k, tn), BF16)],
        compiler_params=_params("parallel", "arbitrary"),
        name="outproj",
    )(*ys, w, x)


def _attn_a_kernel(q_ref, k_ref, v_ref, g_ref, o_ref, qs_ref, v1_ref, s_ref, p_ref, m_ref, alpha_ref,
                   acc_ref, kmax_ref, *, tk, group, chunk):
    tq = q_ref.shape[0]
    s_len = k_ref.shape[0]
    nk = s_len // tk

    @pl.when(pl.program_id(1) == 0)
    def _():
        v1_ref[:, :LANES] = v_ref[...]
        v1_ref[:, LANES:] = jnp.ones((s_len, LANES), BF16)
        kf = k_ref[...].astype(F32)
        kmax_ref[0] = jnp.max(jnp.sum(kf * kf, axis=-1, keepdims=True))

    for g in range(group):
        qs_ref[g * tq:(g + 1) * tq, :] = q_ref[:, g * LANES:(g + 1) * LANES]
    rows = group * tq

    def scores(t):
        k = k_ref[t * tk:(t + 1) * tk, :]
        return lax.dot_general(qs_ref[...], k, (((1,), (1,)), ((), ())), preferred_element_type=F32)

    qf = qs_ref[...].astype(F32)
    bound = jnp.sqrt(jnp.sum(qf * qf, axis=-1, keepdims=True) * kmax_ref[0])

    def fixed_shift():
        acc_ref[...] = jnp.zeros((rows, 2 * LANES), F32)
        for t in range(nk):
            p = jnp.exp2(scores(t) - bound).astype(BF16)
            acc_ref[...] += jnp.dot(p, v1_ref[t * tk:(t + 1) * tk, :], preferred_element_type=F32)

    def running_max():
        m_ref[...] = jnp.full((rows, 1), NEG_INF, F32)
        acc_ref[...] = jnp.zeros((rows, 2 * LANES), F32)
        s_ref[0] = scores(0)
        for t in range(nk):
            if t + 1 < nk:
                s_ref[(t + 1) % 2] = scores(t + 1)
            for c in range(rows // chunk):
                rs = slice(c * chunk, (c + 1) * chunk)
                s = s_ref[t % 2, rs, :]
                m_old = m_ref[rs, :]
                m_new = jnp.maximum(m_old, jnp.max(s, axis=-1, keepdims=True))
                m_ref[rs, :] = m_new
                alpha_ref[rs, :] = jnp.exp2(m_old - m_new)
                p_ref[t % 2, rs, :] = jnp.exp2(s - m_new).astype(BF16)
            pv = jnp.dot(p_ref[t % 2], v1_ref[t * tk:(t + 1) * tk, :], preferred_element_type=F32)
            acc_ref[...] = alpha_ref[...] * acc_ref[...] + pv

    lax.cond(jnp.max(bound) <= A_SAFE_SHIFT, fixed_shift, running_max)
    o = acc_ref[:, :LANES] / acc_ref[:, LANES:]
    for g in range(group):
        sl = slice(g * LANES, (g + 1) * LANES)
        o_ref[:, sl] = (o[g * tq:(g + 1) * tq, :] * g_ref[:, sl]).astype(o_ref.dtype)


def _attn_a(q, k, v, gate, cfg):
    (q_arr, q_col0), (k_arr, k_col0), (v_arr, v_col0) = q, k, v
    s_len = q_arr.shape[0]
    group = cfg.a_heads // cfg.a_kv_heads
    gw = group * LANES
    tq = min(cfg.tq_a, s_len)
    tk = min(cfg.tk_a, s_len)
    return pl.pallas_call(
        functools.partial(_attn_a_kernel, tk=tk, group=group, chunk=cfg.chunk_a),
        grid=(cfg.a_kv_heads, s_len // tq),
        scratch_shapes=[pltpu.VMEM((group * tq, LANES), BF16),
                        pltpu.VMEM((s_len, 2 * LANES), BF16),
                        pltpu.VMEM((2, group * tq, tk), F32),
                        pltpu.VMEM((2, group * tq, tk), BF16),
                        pltpu.VMEM((group * tq, 1), F32),
                        pltpu.VMEM((group * tq, 1), F32),
                        pltpu.VMEM((group * tq, 2 * LANES), F32),
                        pltpu.SMEM((1,), F32)],
        in_specs=[pl.BlockSpec((tq, gw), lambda kv, i: (i, q_col0 // gw + kv)),
                  pl.BlockSpec((s_len, LANES), lambda kv, i: (0, k_col0 // LANES + kv)),
                  pl.BlockSpec((s_len, LANES), lambda kv, i: (0, v_col0 // LANES + kv)),
                  pl.BlockSpec((tq, gw), lambda kv, i: (i, kv))],
        out_specs=pl.BlockSpec((tq, gw), lambda kv, i: (i, kv)),
        out_shape=jax.ShapeDtypeStruct((s_len, cfg.a_heads * LANES), BF16),
        compiler_params=_params("parallel", "arbitrary"),
        name="attn_global",
    )(q_arr, k_arr, v_arr, gate)


def _natten_geometry(rows):
    types, block_type = [], []
    for b in range(rows // NA_QB_ROWS):
        r = b * NA_QB_ROWS + np.arange(NA_QB_ROWS)
        u0 = int(np.clip(r[0] - NA_ROWS // 2, 0, rows - NA_UNION_ROWS))
        rs = np.clip(r - NA_ROWS // 2, 0, rows - NA_ROWS)
        assert u0 <= rs.min() and rs.max() + NA_ROWS <= u0 + NA_UNION_ROWS
        kr = u0 + np.arange(NA_UNION_ROWS)
        vr = (kr[None, :] >= rs[:, None]) & (kr[None, :] < rs[:, None] + NA_ROWS)
        dr = np.where(vr, kr[None, :] - r[:, None] + (NA_ROWS - 1), 0)
        for t, (dr_t, vr_t) in enumerate(types):
            if np.array_equal(dr, dr_t) and np.array_equal(vr, vr_t):
                block_type.append(t)
                break
        else:
            block_type.append(len(types))
            types.append((dr, vr))
    return types, block_type


def _build_natten_tables(rpb_ref, tbl_ref, types):
    c = lax.broadcasted_iota(jnp.int32, (GRID_W, LANES), 0)
    lane = lax.broadcasted_iota(jnp.int32, (GRID_W, LANES), 1)
    first = lane < GRID_W
    kc = jnp.where(first, lane, lane - GRID_W)
    cs = jnp.clip(c - NA_COLS // 2, 0, GRID_W - NA_COLS)
    vc = (kc >= cs) & (kc < cs + NA_COLS)
    base_shift = LANES - (NA_COLS - 1)
    for t, (dr, vr) in enumerate(types):
        for i in range(NA_QB_ROWS):
            for kp in range(NA_UNION_ROWS // 2):
                halves = []
                for half in range(2):
                    ku = 2 * kp + half
                    if vr[i, ku]:
                        row = jnp.broadcast_to(rpb_ref[int(dr[i, ku]):int(dr[i, ku]) + 1, :],
                                               (GRID_W, LANES))
                        halves.append(pltpu.roll(row, (base_shift + half * GRID_W) % LANES, 1,
                                                 stride=1, stride_axis=0))
                    else:
                        halves.append(None)
                lo, hi = halves
                if lo is None and hi is None:
                    tile = jnp.full((GRID_W, LANES), NEG_INF, F32)
                else:
                    ok = vc
                    if lo is None:
                        val, ok = hi, vc & ~first
                    elif hi is None:
                        val, ok = lo, vc & first
                    else:
                        val = jnp.where(first, lo, hi)
                    tile = jnp.where(ok, val * LOG2E, NEG_INF)
                tbl_ref[t, i * GRID_W:(i + 1) * GRID_W, kp * LANES:(kp + 1) * LANES] = tile


def _natten_kernel(q_ref, k_ref, v_ref, g_ref, rpb_ref, o_ref, tbl_ref, v1_ref, *, rows):
    qb = NA_QB_ROWS * GRID_W
    span = NA_UNION_ROWS * GRID_W
    nb_step = q_ref.shape[0] // qb
    types, block_type = _natten_geometry(rows)
    common = max(set(block_type), key=block_type.count)
    step = pl.program_id(1)

    @pl.when(step == 0)
    def _():
        _build_natten_tables(rpb_ref, tbl_ref, types)
        v1_ref[:, :LANES] = v_ref[...]
        v1_ref[:, LANES:] = jnp.ones((v_ref.shape[0], LANES), BF16)

    for b in range(nb_step):
        blk = step * nb_step + b
        u0 = jnp.clip(blk * NA_QB_ROWS - NA_ROWS // 2, 0, rows - NA_UNION_ROWS)
        typ = common
        for bb, t in enumerate(block_type):
            if t != common:
                typ = jnp.where(blk == bb, t, typ)
        koff = pl.multiple_of(u0 * GRID_W, GRID_W)
        qs = slice(b * qb, (b + 1) * qb)
        k = k_ref[pl.ds(koff, span), :]
        v1 = v1_ref[pl.ds(koff, span), :]
        s = lax.dot_general(q_ref[qs, :], k, (((1,), (1,)), ((), ())), preferred_element_type=F32)
        s = s + tbl_ref[typ]
        m = jnp.max(s, axis=-1, keepdims=True)
        pv = jnp.dot(jnp.exp2(s - m).astype(BF16), v1, preferred_element_type=F32)
        o_ref[qs, :] = (pv[:, :LANES] / pv[:, LANES:] * g_ref[qs, :]).astype(o_ref.dtype)


def _natten(q, k, v, gate, gate_col0, rpb, cfg):
    (q_arr, q_col0), (k_arr, k_col0), (v_arr, v_col0) = q, k, v
    s_len = q_arr.shape[0]
    rows = s_len // GRID_W
    tq = min(cfg.tq_b, s_len)
    qb = NA_QB_ROWS * GRID_W
    span = NA_UNION_ROWS * GRID_W
    assert rows % NA_QB_ROWS == 0 and rows >= NA_UNION_ROWS and tq % qb == 0
    n_types = len(_natten_geometry(rows)[0])
    nh, nr, nc = rpb.shape
    rpb_pad = jnp.pad(rpb.astype(F32), ((0, 0), (0, 2 * NA_ROWS - nr), (0, LANES - nc)))
    return pl.pallas_call(
        functools.partial(_natten_kernel, rows=rows),
        grid=(cfg.b_heads, s_len // tq),
        in_specs=[pl.BlockSpec((tq, LANES), lambda h, i: (i, q_col0 // LANES + h)),
                  pl.BlockSpec((s_len, LANES), lambda h, i: (0, k_col0 // LANES + h)),
                  pl.BlockSpec((s_len, LANES), lambda h, i: (0, v_col0 // LANES + h)),
                  pl.BlockSpec((tq, LANES), lambda h, i: (i, gate_col0 // LANES + h)),
                  pl.BlockSpec((None, 2 * NA_ROWS, LANES), lambda h, i: (h, 0, 0))],
        out_specs=pl.BlockSpec((tq, LANES), lambda h, i: (i, h)),
        out_shape=jax.ShapeDtypeStruct((s_len, cfg.b_heads * LANES), BF16),
        scratch_shapes=[pltpu.VMEM((n_types, qb, span), F32),
                        pltpu.VMEM((s_len, 2 * LANES), BF16)],
        compiler_params=_params("parallel", "arbitrary"),
        name="attn_neighbourhood",
    )(q_arr, k_arr, v_arr, gate, rpb_pad)


def _attn_c_kernel(slope_ref, sink_ref, q_ref, k_ref, v_ref, g_ref, o_ref, v1_ref, *, group, sub):
    tq = q_ref.shape[0]
    s_len = k_ref.shape[0]
    span = sub + 2 * C_WINDOW
    kv = pl.program_id(0)

    @pl.when(pl.program_id(1) == 0)
    def _():
        v1_ref[:, :LANES] = v_ref[...]
        v1_ref[:, LANES:] = jnp.ones((s_len, LANES), BF16)

    slopes = [slope_ref[kv * group + g] * LOG2E for g in range(group)]
    sinks = [sink_ref[kv * group + g] * LOG2E for g in range(group)]
    row = lax.broadcasted_iota(jnp.int32, (sub, span), 0)
    col = lax.broadcasted_iota(jnp.int32, (sub, span), 1)
    for b in range(tq // sub):
        t0 = pl.program_id(1) * tq + b * sub
        start = pl.multiple_of(jnp.clip(t0 - C_WINDOW, 0, s_len - span), C_WINDOW)
        k = k_ref[pl.ds(start, span), :]
        v1 = v1_ref[pl.ds(start, span), :]
        rows = slice(b * sub, (b + 1) * sub)
        q = jnp.concatenate([q_ref[rows, g * LANES:(g + 1) * LANES] for g in range(group)], axis=0)
        s = lax.dot_general(q, k, (((1,), (1,)), ((), ())), preferred_element_type=F32)
        dist = jnp.abs(row - col + (t0 - start))
        pen = jnp.where(dist <= C_WINDOW, dist.astype(F32), -NEG_INF)
        ps, ms = [], []
        for g in range(group):
            sg = s[g * sub:(g + 1) * sub, :] - slopes[g] * pen
            m = jnp.maximum(jnp.max(sg, axis=-1, keepdims=True), sinks[g])
            ps.append(jnp.exp2(sg - m).astype(BF16))
            ms.append(m)
        pv = jnp.dot(jnp.concatenate(ps, axis=0), v1, preferred_element_type=F32)
        for g in range(group):
            sl = slice(g * LANES, (g + 1) * LANES)
            pg = pv[g * sub:(g + 1) * sub, :]
            l = pg[:, LANES:] + jnp.exp2(sinks[g] - ms[g])
            o_ref[rows, sl] = (pg[:, :LANES] / l * g_ref[rows, sl]).astype(o_ref.dtype)


def _attn_c(q, k, v, gate, slopes, sink, cfg):
    (q_arr, q_col0), (k_arr, k_col0), (v_arr, v_col0) = q, k, v
    s_len = q_arr.shape[0]
    group = cfg.c_heads // cfg.c_kv_heads
    gw = group * LANES
    tq, sub = cfg.tq_c, cfg.sub_c
    assert s_len >= sub + 2 * C_WINDOW and s_len % tq == 0 and tq % sub == 0 and sub % C_WINDOW == 0
    smem = pl.BlockSpec(memory_space=pltpu.SMEM)
    return pl.pallas_call(
        functools.partial(_attn_c_kernel, group=group, sub=sub),
        grid=(cfg.c_kv_heads, s_len // tq),
        scratch_shapes=[pltpu.VMEM((s_len, 2 * LANES), BF16)],
        in_specs=[smem, smem,
                  pl.BlockSpec((tq, gw), lambda kv, i: (i, q_col0 // gw + kv)),
                  pl.BlockSpec((s_len, LANES), lambda kv, i: (0, k_col0 // LANES + kv)),
                  pl.BlockSpec((s_len, LANES), lambda kv, i: (0, v_col0 // LANES + kv)),
                  pl.BlockSpec((tq, gw), lambda kv, i: (i, kv))],
        out_specs=pl.BlockSpec((tq, gw), lambda kv, i: (i, kv)),
        out_shape=jax.ShapeDtypeStruct((s_len, cfg.c_heads * LANES), BF16),
        compiler_params=_params("parallel", "arbitrary"),
        name="attn_window",
    )(slopes, sink, q_arr, k_arr, v_arr, gate)


def _rope_tables(s_len):
    half = LANES // 2
    quarter = half // 2
    inv = jnp.exp(-math.log(ROPE_THETA) * jnp.arange(0, half, 2, dtype=F32) / half)
    pos = jnp.arange(s_len, dtype=jnp.int32)
    rows = (pos // GRID_W).astype(F32)
    cols = (pos % GRID_W).astype(F32)
    ang = jnp.concatenate([rows[:, None] * inv[None, :]] * 2 + [cols[:, None] * inv[None, :]] * 2, axis=1)
    first = (np.arange(LANES) % half) < quarter
    cos, sin = jnp.cos(ang), jnp.sin(ang)
    return cos, jnp.where(first[None], -sin, 0.0), jnp.where(first[None], 0.0, sin)


def _forward(x, norm_w, w_in_ab, w_out_ab, q_norm_a, k_norm_a, rpb_b, w_in_c, w_out_c, sink_c,
             final_norm_w, cfg):
    bsz, s_len, d = x.shape
    assert bsz == 1 and s_len % GRID_W == 0
    depth = norm_w.shape[0]
    scale = LANES ** -0.5 * LOG2E
    a_q, a_kv, b_w = cfg.a_heads * LANES, cfg.a_kv_heads * LANES, cfg.b_heads * LANES
    c_q, c_kv = cfg.c_heads * LANES, cfg.c_kv_heads * LANES
    rope = _rope_tables(s_len)
    slopes = jnp.exp2(-8.0 * jnp.arange(1, cfg.c_heads + 1, dtype=F32) / cfg.c_heads)
    xs = x.reshape(s_len, d)
    for layer in range(depth):
        h = _rmsnorm(xs, norm_w[layer], BF16, cfg.t_norm)
        if layer % 2 == 0:
            e = layer // 2
            n_rope = a_q + a_kv
            zr = _rope_proj(h, w_in_ab, e, a_q, a_kv, scale, rope, q_norm_a[e], k_norm_a[e], cfg)
            zc = _cast_proj(h, w_in_ab, e, n_rope, ((a_kv, 1.0), (b_w, scale), (2 * b_w, 1.0)), cfg)
            gate = _gate_proj(h, w_in_ab, e, n_rope + a_kv + 3 * b_w, a_q + b_w, cfg)
            ya = _attn_a((zr, 0), (zr, a_q), (zc, 0), gate, cfg)
            yb = _natten((zc, a_kv), (zc, a_kv + b_w), (zc, a_kv + 2 * b_w), gate, a_q, rpb_b[e], cfg)
            xs = _outproj([ya, yb], w_out_ab, e, xs, cfg)
        else:
            o = layer // 2
            zc = _cast_proj(h, w_in_c, o, 0, ((c_q, scale), (2 * c_kv, 1.0)), cfg)
            gate = _gate_proj(h, w_in_c, o, c_q + 2 * c_kv, c_q, cfg)
            y = _attn_c((zc, 0), (zc, c_q), (zc, c_q + c_kv), gate, slopes, sink_c[o].astype(F32), cfg)
            xs = _outproj([y], w_out_c, o, xs, cfg)
    out = _rmsnorm(xs, final_norm_w, x.dtype, cfg.t_norm)
    return out.reshape(bsz, s_len, d)


def kernel(x, norm_w, w_in_ab, w_out_ab, q_norm_a, k_norm_a, rpb_b, w_in_c, w_out_c, sink_c, final_norm_w):
    return _forward(x, norm_w, w_in_ab, w_out_ab, q_norm_a, k_norm_a, rpb_b, w_in_c, w_out_c, sink_c,
                    final_norm_w, Config())
```

```python
import functools
import math
from typing import NamedTuple

import jax
import jax.numpy as jnp
import numpy as np
from jax import lax
from jax.experimental import pallas as pl
from jax.experimental.pallas import tpu as pltpu

F32 = jnp.float32
BF16 = jnp.bfloat16

LANES = 128
NORM_EPS = 1e-6
NEG_INF = -1e30
LOG2E = math.log2(math.e)
ROPE_THETA = 10000.0
GRID_W = 64
NA_ROWS = 8
NA_COLS = 16
NA_QB_ROWS = 4
NA_UNION_ROWS = 12
C_WINDOW = 128
PROJ_SPLIT = 4
A_SAFE_SHIFT = 60.0
VMEM_LIMIT_BYTES = 56 * 1024 * 1024


class Config(NamedTuple):
    a_heads: int = 16
    a_kv_heads: int = 4
    b_heads: int = 16
    c_heads: int = 32
    c_kv_heads: int = 8
    tm: int = 1024
    tn: int = 512
    tq_a: int = 256
    tk_a: int = 1024
    chunk_a: int = 16
    tq_c: int = 512
    sub_c: int = 128
    tq_b: int = 2048
    t_norm: int = 256


def _params(*sem):
    return pltpu.CompilerParams(dimension_semantics=sem, vmem_limit_bytes=VMEM_LIMIT_BYTES)


def _rmsnorm_kernel(x_ref, w_ref, o_ref):
    x = x_ref[...]
    ms = jnp.mean(x * x, axis=-1, keepdims=True)
    o_ref[...] = (x * lax.rsqrt(ms + NORM_EPS) * w_ref[...]).astype(o_ref.dtype)


def _rmsnorm(x, w, out_dtype, t):
    m, d = x.shape
    return pl.pallas_call(
        _rmsnorm_kernel,
        grid=(m // t,),
        in_specs=[pl.BlockSpec((t, d), lambda i: (i, 0)),
                  pl.BlockSpec((1, d), lambda i: (0, 0))],
        out_specs=pl.BlockSpec((t, d), lambda i: (i, 0)),
        out_shape=jax.ShapeDtypeStruct((m, d), out_dtype),
        compiler_params=_params("parallel"),
        name="rmsnorm",
    )(x, w.reshape(1, d))


def _weight_copy(w_hbm, wbuf_ref, sem_ref, e, tile, slot):
    tn = wbuf_ref.shape[2]
    cols = pl.ds(pl.multiple_of(tile * tn, tn), tn)
    return pltpu.make_async_copy(w_hbm.at[e, :, cols], wbuf_ref.at[slot], sem_ref.at[slot])


def _stage_weight(w_hbm, wbuf_ref, sem_ref, wbf_ref, e, tile0):
    j, nj = pl.program_id(0), pl.num_programs(0)

    @pl.when(pl.program_id(1) == 0)
    def _():
        slot = j % 2

        @pl.when(j == 0)
        def _():
            _weight_copy(w_hbm, wbuf_ref, sem_ref, e, tile0, 0).start()

        @pl.when(j + 1 < nj)
        def _():
            _weight_copy(w_hbm, wbuf_ref, sem_ref, e, tile0 + j + 1, 1 - slot).start()

        _weight_copy(w_hbm, wbuf_ref, sem_ref, e, tile0 + j, slot).wait()
        wbf_ref[...] = wbuf_ref[slot].astype(BF16)


def _weight_scratch(k, tn):
    return [pltpu.VMEM((2, k, tn), F32), pltpu.SemaphoreType.DMA((2,)), pltpu.VMEM((k, tn), BF16)]


W_HBM_SPEC = pl.BlockSpec(memory_space=pl.ANY)


def _norm_rope(z, w, cos, sin_lo, sin_hi):
    ms = jnp.mean(z * z, axis=-1, keepdims=True)
    y = z * lax.rsqrt(ms + NORM_EPS) * w
    quarter = LANES // 4
    return (y * cos + pltpu.roll(y, LANES - quarter, 1) * sin_lo
            + pltpu.roll(y, quarter, 1) * sin_hi)


def _row_blocks(tm):
    step = tm // PROJ_SPLIT
    return [slice(r * step, (r + 1) * step) for r in range(PROJ_SPLIT)]


def _rope_proj_kernel(h_ref, w_hbm, cos_ref, slo_ref, shi_ref, qn_ref, kn_ref, o_ref, wbuf_ref, sem_ref,
                      wbf_ref, *, e, q_tiles, q_scale):
    _stage_weight(w_hbm, wbuf_ref, sem_ref, wbf_ref, e, 0)
    nw = jnp.where(pl.program_id(0) < q_tiles, qn_ref[...] * q_scale, kn_ref[...])
    for rs in _row_blocks(h_ref.shape[0]):
        acc = jnp.dot(h_ref[rs, :], wbf_ref[...], preferred_element_type=F32)
        cos, slo, shi = cos_ref[rs, :], slo_ref[rs, :], shi_ref[rs, :]
        for hh in range(acc.shape[1] // LANES):
            sl = slice(hh * LANES, (hh + 1) * LANES)
            o_ref[rs, sl] = _norm_rope(acc[:, sl], nw, cos, slo, shi).astype(o_ref.dtype)


def _rope_proj(h, w, e, n_q, n_k, q_scale, rope, qn, kn, cfg):
    m, k = h.shape
    tm, tn = min(cfg.tm, m), cfg.tn
    row = lambda j, i: (i, 0)
    vec = pl.BlockSpec((1, LANES), lambda j, i: (0, 0))
    return pl.pallas_call(
        functools.partial(_rope_proj_kernel, e=e, q_tiles=n_q // tn, q_scale=q_scale),
        grid=((n_q + n_k) // tn, m // tm),
        in_specs=[pl.BlockSpec((tm, k), row), W_HBM_SPEC,
                  pl.BlockSpec((tm, LANES), row), pl.BlockSpec((tm, LANES), row),
                  pl.BlockSpec((tm, LANES), row), vec, vec],
        out_specs=pl.BlockSpec((tm, tn), lambda j, i: (i, j)),
        out_shape=jax.ShapeDtypeStruct((m, n_q + n_k), BF16),
        scratch_shapes=_weight_scratch(k, tn),
        compiler_params=_params("arbitrary", "arbitrary"),
        name="ropeproj",
    )(h, w, *rope, qn.reshape(1, LANES), kn.reshape(1, LANES))


def _cast_proj_kernel(scale_ref, h_ref, w_hbm, o_ref, wbuf_ref, sem_ref, wbf_ref, *, e, tile0):
    _stage_weight(w_hbm, wbuf_ref, sem_ref, wbf_ref, e, tile0)
    scale = scale_ref[pl.program_id(0)]
    for rs in _row_blocks(h_ref.shape[0]):
        acc = jnp.dot(h_ref[rs, :], wbf_ref[...], preferred_element_type=F32)
        o_ref[rs, :] = (acc * scale).astype(o_ref.dtype)


def _cast_proj(h, w, e, col0, widths_scales, cfg):
    m, k = h.shape
    tm, tn = min(cfg.tm, m), cfg.tn
    scales = np.concatenate([np.full(width // tn, scale, np.float32) for width, scale in widths_scales])
    n_out = tn * len(scales)
    return pl.pallas_call(
        functools.partial(_cast_proj_kernel, e=e, tile0=col0 // tn),
        grid=(n_out // tn, m // tm),
        in_specs=[pl.BlockSpec(memory_space=pltpu.SMEM),
                  pl.BlockSpec((tm, k), lambda j, i: (i, 0)), W_HBM_SPEC],
        out_specs=pl.BlockSpec((tm, tn), lambda j, i: (i, j)),
        out_shape=jax.ShapeDtypeStruct((m, n_out), BF16),
        scratch_shapes=_weight_scratch(k, tn),
        compiler_params=_params("arbitrary", "arbitrary"),
        name="castproj",
    )(jnp.asarray(scales), h, w)


def _gate_kernel(h_ref, w_hbm, o_ref, wbuf_ref, sem_ref, wbf_ref, *, e, tile0):
    _stage_weight(w_hbm, wbuf_ref, sem_ref, wbf_ref, e, tile0)
    for rs in _row_blocks(h_ref.shape[0]):
        z = jnp.dot(h_ref[rs, :], wbf_ref[...], preferred_element_type=F32)
        o_ref[rs, :] = (z * (0.5 * jnp.tanh(0.5 * z) + 0.5)).astype(o_ref.dtype)


def _gate_proj(h, w, e, col0, n_out, cfg):
    m, k = h.shape
    tm, tn = min(cfg.tm, m), cfg.tn
    return pl.pallas_call(
        functools.partial(_gate_kernel, e=e, tile0=col0 // tn),
        grid=(n_out // tn, m // tm),
        in_specs=[pl.BlockSpec((tm, k), lambda j, i: (i, 0)), W_HBM_SPEC],
        out_specs=pl.BlockSpec((tm, tn), lambda j, i: (i, j)),
        out_shape=jax.ShapeDtypeStruct((m, n_out), BF16),
        scratch_shapes=_weight_scratch(k, tn),
        compiler_params=_params("arbitrary", "arbitrary"),
        name="gateproj",
    )(h, w)


def _outproj_kernel(*refs, n_lhs, e):
    y_refs = refs[:n_lhs]
    w_hbm, x_ref, o_ref, wbuf_ref, sem_ref, wbf_ref = refs[n_lhs:]
    _stage_weight(w_hbm, wbuf_ref, sem_ref, wbf_ref, e, 0)
    acc = x_ref[...]
    k0 = 0
    for y_ref in y_refs:
        kk = y_ref.shape[1]
        acc = acc + jnp.dot(y_ref[...], wbf_ref[k0:k0 + kk, :], preferred_element_type=F32)
        k0 += kk
    o_ref[...] = acc


def _outproj(ys, w, e, x, cfg):
    m, n = x.shape
    k = w.shape[1]
    tm, tn = min(cfg.tm, m), cfg.tn
    in_specs = [pl.BlockSpec((tm, y.shape[1]), lambda j, i: (i, 0)) for y in ys]
    in_specs += [W_HBM_SPEC, pl.BlockSpec((tm, tn), lambda j, i: (i, j))]
    return pl.pallas_call(
        functools.partial(_outproj_kernel, n_lhs=len(ys), e=e),
        grid=(n // tn, m // tm),
        in_specs=in_specs,
        out_specs=pl.BlockSpec((tm, tn), lambda j, i: (i, j)),
        out_shape=jax.ShapeDtypeStruct((m, n), F32),
        scratch_shapes=_weight_scratch(k, tn),
        compiler_params=_params("arbitrary", "arbitrary"),
        name="outproj",
    )(*ys, w, x)


def _attn_a_kernel(q_ref, k_ref, v_ref, g_ref, o_ref, qs_ref, v1_ref, s_ref, p_ref, m_ref, alpha_ref,
                   acc_ref, kmax_ref, *, tk, group, chunk):
    tq = q_ref.shape[0]
    s_len = k_ref.shape[0]
    nk = s_len // tk

    @pl.when(pl.program_id(1) == 0)
    def _():
        v1_ref[:, :LANES] = v_ref[...]
        v1_ref[:, LANES:] = jnp.ones((s_len, LANES), BF16)
        kf = k_ref[...].astype(F32)
        kmax_ref[0] = jnp.max(jnp.sum(kf * kf, axis=-1, keepdims=True))

    for g in range(group):
        qs_ref[g * tq:(g + 1) * tq, :] = q_ref[:, g * LANES:(g + 1) * LANES]
    rows = group * tq

    def scores(t):
        k = k_ref[t * tk:(t + 1) * tk, :]
        return lax.dot_general(qs_ref[...], k, (((1,), (1,)), ((), ())), preferred_element_type=F32)

    qf = qs_ref[...].astype(F32)
    bound = jnp.sqrt(jnp.sum(qf * qf, axis=-1, keepdims=True) * kmax_ref[0])

    def fixed_shift():
        acc_ref[...] = jnp.zeros((rows, 2 * LANES), F32)
        for t in range(nk):
            p = jnp.exp2(scores(t) - bound).astype(BF16)
            acc_ref[...] += jnp.dot(p, v1_ref[t * tk:(t + 1) * tk, :], preferred_element_type=F32)

    def running_max():
        m_ref[...] = jnp.full((rows, 1), NEG_INF, F32)
        acc_ref[...] = jnp.zeros((rows, 2 * LANES), F32)
        s_ref[0] = scores(0)
        for t in range(nk):
            if t + 1 < nk:
                s_ref[(t + 1) % 2] = scores(t + 1)
            for c in range(rows // chunk):
                rs = slice(c * chunk, (c + 1) * chunk)
                s = s_ref[t % 2, rs, :]
                m_old = m_ref[rs, :]
                m_new = jnp.maximum(m_old, jnp.max(s, axis=-1, keepdims=True))
                m_ref[rs, :] = m_new
                alpha_ref[rs, :] = jnp.exp2(m_old - m_new)
                p_ref[t % 2, rs, :] = jnp.exp2(s - m_new).astype(BF16)
            pv = jnp.dot(p_ref[t % 2], v1_ref[t * tk:(t + 1) * tk, :], preferred_element_type=F32)
            acc_ref[...] = alpha_ref[...] * acc_ref[...] + pv

    lax.cond(jnp.max(bound) <= A_SAFE_SHIFT, fixed_shift, running_max)
    o = acc_ref[:, :LANES] / acc_ref[:, LANES:]
    for g in range(group):
        sl = slice(g * LANES, (g + 1) * LANES)
        o_ref[:, sl] = (o[g * tq:(g + 1) * tq, :] * g_ref[:, sl]).astype(o_ref.dtype)


def _attn_a(q, k, v, gate, cfg):
    (q_arr, q_col0), (k_arr, k_col0), (v_arr, v_col0) = q, k, v
    s_len = q_arr.shape[0]
    group = cfg.a_heads // cfg.a_kv_heads
    gw = group * LANES
    tq = min(cfg.tq_a, s_len)
    tk = min(cfg.tk_a, s_len)
    return pl.pallas_call(
        functools.partial(_attn_a_kernel, tk=tk, group=group, chunk=cfg.chunk_a),
        grid=(cfg.a_kv_heads, s_len // tq),
        scratch_shapes=[pltpu.VMEM((group * tq, LANES), BF16),
                        pltpu.VMEM((s_len, 2 * LANES), BF16),
                        pltpu.VMEM((2, group * tq, tk), F32),
                        pltpu.VMEM((2, group * tq, tk), BF16),
                        pltpu.VMEM((group * tq, 1), F32),
                        pltpu.VMEM((group * tq, 1), F32),
                        pltpu.VMEM((group * tq, 2 * LANES), F32),
                        pltpu.SMEM((1,), F32)],
        in_specs=[pl.BlockSpec((tq, gw), lambda kv, i: (i, q_col0 // gw + kv)),
                  pl.BlockSpec((s_len, LANES), lambda kv, i: (0, k_col0 // LANES + kv)),
                  pl.BlockSpec((s_len, LANES), lambda kv, i: (0, v_col0 // LANES + kv)),
                  pl.BlockSpec((tq, gw), lambda kv, i: (i, kv))],
        out_specs=pl.BlockSpec((tq, gw), lambda kv, i: (i, kv)),
        out_shape=jax.ShapeDtypeStruct((s_len, cfg.a_heads * LANES), BF16),
        compiler_params=_params("parallel", "arbitrary"),
        name="attn_global",
    )(q_arr, k_arr, v_arr, gate)


def _natten_geometry(rows):
    types, block_type = [], []
    for b in range(rows // NA_QB_ROWS):
        r = b * NA_QB_ROWS + np.arange(NA_QB_ROWS)
        u0 = int(np.clip(r[0] - NA_ROWS // 2, 0, rows - NA_UNION_ROWS))
        rs = np.clip(r - NA_ROWS // 2, 0, rows - NA_ROWS)
        assert u0 <= rs.min() and rs.max() + NA_ROWS <= u0 + NA_UNION_ROWS
        kr = u0 + np.arange(NA_UNION_ROWS)
        vr = (kr[None, :] >= rs[:, None]) & (kr[None, :] < rs[:, None] + NA_ROWS)
        dr = np.where(vr, kr[None, :] - r[:, None] + (NA_ROWS - 1), 0)
        for t, (dr_t, vr_t) in enumerate(types):
            if np.array_equal(dr, dr_t) and np.array_equal(vr, vr_t):
                block_type.append(t)
                break
        else:
            block_type.append(len(types))
            types.append((dr, vr))
    return types, block_type


def _build_natten_tables(rpb_ref, tbl_ref, types):
    c = lax.broadcasted_iota(jnp.int32, (GRID_W, LANES), 0)
    lane = lax.broadcasted_iota(jnp.int32, (GRID_W, LANES), 1)
    first = lane < GRID_W
    kc = jnp.where(first, lane, lane - GRID_W)
    cs = jnp.clip(c - NA_COLS // 2, 0, GRID_W - NA_COLS)
    vc = (kc >= cs) & (kc < cs + NA_COLS)
    base_shift = LANES - (NA_COLS - 1)
    for t, (dr, vr) in enumerate(types):
        for i in range(NA_QB_ROWS):
            for kp in range(NA_UNION_ROWS // 2):
                halves = []
                for half in range(2):
                    ku = 2 * kp + half
                    if vr[i, ku]:
                        row = jnp.broadcast_to(rpb_ref[int(dr[i, ku]):int(dr[i, ku]) + 1, :],
                                               (GRID_W, LANES))
                        halves.append(pltpu.roll(row, (base_shift + half * GRID_W) % LANES, 1,
                                                 stride=1, stride_axis=0))
                    else:
                        halves.append(None)
                lo, hi = halves
                if lo is None and hi is None:
                    tile = jnp.full((GRID_W, LANES), NEG_INF, F32)
                else:
                    ok = vc
                    if lo is None:
                        val, ok = hi, vc & ~first
                    elif hi is None:
                        val, ok = lo, vc & first
                    else:
                        val = jnp.where(first, lo, hi)
                    tile = jnp.where(ok, val * LOG2E, NEG_INF)
                tbl_ref[t, i * GRID_W:(i + 1) * GRID_W, kp * LANES:(kp + 1) * LANES] = tile


def _natten_kernel(q_ref, k_ref, v_ref, g_ref, rpb_ref, o_ref, tbl_ref, v1_ref, *, rows):
    qb = NA_QB_ROWS * GRID_W
    span = NA_UNION_ROWS * GRID_W
    nb_step = q_ref.shape[0] // qb
    types, block_type = _natten_geometry(rows)
    common = max(set(block_type), key=block_type.count)
    step = pl.program_id(1)

    @pl.when(step == 0)
    def _():
        _build_natten_tables(rpb_ref, tbl_ref, types)
        v1_ref[:, :LANES] = v_ref[...]
        v1_ref[:, LANES:] = jnp.ones((v_ref.shape[0], LANES), BF16)

    for b in range(nb_step):
        blk = step * nb_step + b
        u0 = jnp.clip(blk * NA_QB_ROWS - NA_ROWS // 2, 0, rows - NA_UNION_ROWS)
        typ = common
        for bb, t in enumerate(block_type):
            if t != common:
                typ = jnp.where(blk == bb, t, typ)
        koff = pl.multiple_of(u0 * GRID_W, GRID_W)
        qs = slice(b * qb, (b + 1) * qb)
        k = k_ref[pl.ds(koff, span), :]
        v1 = v1_ref[pl.ds(koff, span), :]
        s = lax.dot_general(q_ref[qs, :], k, (((1,), (1,)), ((), ())), preferred_element_type=F32)
        s = s + tbl_ref[typ]
        m = jnp.max(s, axis=-1, keepdims=True)
        pv = jnp.dot(jnp.exp2(s - m).astype(BF16), v1, preferred_element_type=F32)
        o_ref[qs, :] = (pv[:, :LANES] / pv[:, LANES:] * g_ref[qs, :]).astype(o_ref.dtype)


def _natten(q, k, v, gate, gate_col0, rpb, cfg):
    (q_arr, q_col0), (k_arr, k_col0), (v_arr, v_col0) = q, k, v
    s_len = q_arr.shape[0]
    rows = s_len // GRID_W
    tq = min(cfg.tq_b, s_len)
    qb = NA_QB_ROWS * GRID_W
    span = NA_UNION_ROWS * GRID_W
    assert rows % NA_QB_ROWS == 0 and rows >= NA_UNION_ROWS and tq % qb == 0
    n_types = len(_natten_geometry(rows)[0])
    nh, nr, nc = rpb.shape
    rpb_pad = jnp.pad(rpb.astype(F32), ((0, 0), (0, 2 * NA_ROWS - nr), (0, LANES - nc)))
    return pl.pallas_call(
        functools.partial(_natten_kernel, rows=rows),
        grid=(cfg.b_heads, s_len // tq),
        in_specs=[pl.BlockSpec((tq, LANES), lambda h, i: (i, q_col0 // LANES + h)),
                  pl.BlockSpec((s_len, LANES), lambda h, i: (0, k_col0 // LANES + h)),
                  pl.BlockSpec((s_len, LANES), lambda h, i: (0, v_col0 // LANES + h)),
                  pl.BlockSpec((tq, LANES), lambda h, i: (i, gate_col0 // LANES + h)),
                  pl.BlockSpec((None, 2 * NA_ROWS, LANES), lambda h, i: (h, 0, 0))],
        out_specs=pl.BlockSpec((tq, LANES), lambda h, i: (i, h)),
        out_shape=jax.ShapeDtypeStruct((s_len, cfg.b_heads * LANES), BF16),
        scratch_shapes=[pltpu.VMEM((n_types, qb, span), F32),
                        pltpu.VMEM((s_len, 2 * LANES), BF16)],
        compiler_params=_params("parallel", "arbitrary"),
        name="attn_neighbourhood",
    )(q_arr, k_arr, v_arr, gate, rpb_pad)


def _attn_c_kernel(slope_ref, sink_ref, q_ref, k_ref, v_ref, g_ref, o_ref, v1_ref, *, group, sub):
    tq = q_ref.shape[0]
    s_len = k_ref.shape[0]
    span = sub + 2 * C_WINDOW
    kv = pl.program_id(0)

    @pl.when(pl.program_id(1) == 0)
    def _():
        v1_ref[:, :LANES] = v_ref[...]
        v1_ref[:, LANES:] = jnp.ones((s_len, LANES), BF16)

    slopes = [slope_ref[kv * group + g] * LOG2E for g in range(group)]
    sinks = [sink_ref[kv * group + g] * LOG2E for g in range(group)]
    row = lax.broadcasted_iota(jnp.int32, (sub, span), 0)
    col = lax.broadcasted_iota(jnp.int32, (sub, span), 1)
    for b in range(tq // sub):
        t0 = pl.program_id(1) * tq + b * sub
        start = pl.multiple_of(jnp.clip(t0 - C_WINDOW, 0, s_len - span), C_WINDOW)
        k = k_ref[pl.ds(start, span), :]
        v1 = v1_ref[pl.ds(start, span), :]
        rows = slice(b * sub, (b + 1) * sub)
        q = jnp.concatenate([q_ref[rows, g * LANES:(g + 1) * LANES] for g in range(group)], axis=0)
        s = lax.dot_general(q, k, (((1,), (1,)), ((), ())), preferred_element_type=F32)
        dist = jnp.abs(row - col + (t0 - start))
        pen = jnp.where(dist <= C_WINDOW, dist.astype(F32), -NEG_INF)
        ps, ms = [], []
        for g in range(group):
            sg = s[g * sub:(g + 1) * sub, :] - slopes[g] * pen
            m = jnp.maximum(jnp.max(sg, axis=-1, keepdims=True), sinks[g])
            ps.append(jnp.exp2(sg - m).astype(BF16))
            ms.append(m)
        pv = jnp.dot(jnp.concatenate(ps, axis=0), v1, preferred_element_type=F32)
        for g in range(group):
            sl = slice(g * LANES, (g + 1) * LANES)
            pg = pv[g * sub:(g + 1) * sub, :]
            l = pg[:, LANES:] + jnp.exp2(sinks[g] - ms[g])
            o_ref[rows, sl] = (pg[:, :LANES] / l * g_ref[rows, sl]).astype(o_ref.dtype)


def _attn_c(q, k, v, gate, slopes, sink, cfg):
    (q_arr, q_col0), (k_arr, k_col0), (v_arr, v_col0) = q, k, v
    s_len = q_arr.shape[0]
    group = cfg.c_heads // cfg.c_kv_heads
    gw = group * LANES
    tq, sub = cfg.tq_c, cfg.sub_c
    assert s_len >= sub + 2 * C_WINDOW and s_len % tq == 0 and tq % sub == 0 and sub % C_WINDOW == 0
    smem = pl.BlockSpec(memory_space=pltpu.SMEM)
    return pl.pallas_call(
        functools.partial(_attn_c_kernel, group=group, sub=sub),
        grid=(cfg.c_kv_heads, s_len // tq),
        scratch_shapes=[pltpu.VMEM((s_len, 2 * LANES), BF16)],
        in_specs=[smem, smem,
                  pl.BlockSpec((tq, gw), lambda kv, i: (i, q_col0 // gw + kv)),
                  pl.BlockSpec((s_len, LANES), lambda kv, i: (0, k_col0 // LANES + kv)),
                  pl.BlockSpec((s_len, LANES), lambda kv, i: (0, v_col0 // LANES + kv)),
                  pl.BlockSpec((tq, gw), lambda kv, i: (i, kv))],
        out_specs=pl.BlockSpec((tq, gw), lambda kv, i: (i, kv)),
        out_shape=jax.ShapeDtypeStruct((s_len, cfg.c_heads * LANES), BF16),
        compiler_params=_params("parallel", "arbitrary"),
        name="attn_window",
    )(slopes, sink, q_arr, k_arr, v_arr, gate)


def _rope_tables(s_len):
    half = LANES // 2
    quarter = half // 2
    inv = jnp.exp(-math.log(ROPE_THETA) * jnp.arange(0, half, 2, dtype=F32) / half)
    pos = jnp.arange(s_len, dtype=jnp.int32)
    rows = (pos // GRID_W).astype(F32)
    cols = (pos % GRID_W).astype(F32)
    ang = jnp.concatenate([rows[:, None] * inv[None, :]] * 2 + [cols[:, None] * inv[None, :]] * 2, axis=1)
    first = (np.arange(LANES) % half) < quarter
    cos, sin = jnp.cos(ang), jnp.sin(ang)
    return cos, jnp.where(first[None], -sin, 0.0), jnp.where(first[None], 0.0, sin)


def _forward(x, norm_w, w_in_ab, w_out_ab, q_norm_a, k_norm_a, rpb_b, w_in_c, w_out_c, sink_c,
             final_norm_w, cfg):
    bsz, s_len, d = x.shape
    assert bsz == 1 and s_len % GRID_W == 0
    depth = norm_w.shape[0]
    scale = LANES ** -0.5 * LOG2E
    a_q, a_kv, b_w = cfg.a_heads * LANES, cfg.a_kv_heads * LANES, cfg.b_heads * LANES
    c_q, c_kv = cfg.c_heads * LANES, cfg.c_kv_heads * LANES
    rope = _rope_tables(s_len)
    slopes = jnp.exp2(-8.0 * jnp.arange(1, cfg.c_heads + 1, dtype=F32) / cfg.c_heads)
    xs = x.reshape(s_len, d)
    for layer in range(depth):
        h = _rmsnorm(xs, norm_w[layer], BF16, cfg.t_norm)
        if layer % 2 == 0:
            e = layer // 2
            n_rope = a_q + a_kv
            zr = _rope_proj(h, w_in_ab, e, a_q, a_kv, scale, rope, q_norm_a[e], k_norm_a[e], cfg)
            zc = _cast_proj(h, w_in_ab, e, n_rope, ((a_kv, 1.0), (b_w, scale), (2 * b_w, 1.0)), cfg)
            gate = _gate_proj(h, w_in_ab, e, n_rope + a_kv + 3 * b_w, a_q + b_w, cfg)
            ya = _attn_a((zr, 0), (zr, a_q), (zc, 0), gate, cfg)
            yb = _natten((zc, a_kv), (zc, a_kv + b_w), (zc, a_kv + 2 * b_w), gate, a_q, rpb_b[e], cfg)
            xs = _outproj([ya, yb], w_out_ab, e, xs, cfg)
        else:
            o = layer // 2
            zc = _cast_proj(h, w_in_c, o, 0, ((c_q, scale), (2 * c_kv, 1.0)), cfg)
            gate = _gate_proj(h, w_in_c, o, c_q + 2 * c_kv, c_q, cfg)
            y = _attn_c((zc, 0), (zc, c_q), (zc, c_q + c_kv), gate, slopes, sink_c[o].astype(F32), cfg)
            xs = _outproj([y], w_out_c, o, xs, cfg)
    out = _rmsnorm(xs, final_norm_w, x.dtype, cfg.t_norm)
    return out.reshape(bsz, s_len, d)


def kernel(x, norm_w, w_in_ab, w_out_ab, q_norm_a, k_norm_a, rpb_b, w_in_c, w_out_c, sink_c, final_norm_w):
    return _forward(x, norm_w, w_in_ab, w_out_ab, q_norm_a, k_norm_a, rpb_b, w_in_c, w_out_c, sink_c,
                    final_norm_w, Config())
```

```python
import functools
import math
from typing import NamedTuple

import jax
import jax.numpy as jnp
import numpy as np
from jax import lax
from jax.experimental import pallas as pl
from jax.experimental.pallas import tpu as pltpu

F32 = jnp.float32
BF16 = jnp.bfloat16

LANES = 128
NORM_EPS = 1e-6
NEG_INF = -1e30
LOG2E = math.log2(math.e)
ROPE_THETA = 10000.0
GRID_W = 64
NA_ROWS = 8
NA_COLS = 16
NA_QB_ROWS = 4
NA_UNION_ROWS = 12
C_WINDOW = 128
PROJ_SPLIT = 4
A_SAFE_SHIFT = 60.0
VMEM_LIMIT_BYTES = 56 * 1024 * 1024


class Config(NamedTuple):
    a_heads: int = 16
    a_kv_heads: int = 4
    b_heads: int = 16
    c_heads: int = 32
    c_kv_heads: int = 8
    tm: int = 1024
    tn: int = 512
    tq_a: int = 256
    tk_a: int = 1024
    chunk_a: int = 16
    tq_c: int = 512
    sub_c: int = 128
    tq_b: int = 2048
    t_norm: int = 256


def _params(*sem):
    return pltpu.CompilerParams(dimension_semantics=sem, vmem_limit_bytes=VMEM_LIMIT_BYTES)


def _rmsnorm_kernel(x_ref, w_ref, o_ref):
    x = x_ref[...]
    ms = jnp.mean(x * x, axis=-1, keepdims=True)
    o_ref[...] = (x * lax.rsqrt(ms + NORM_EPS) * w_ref[...]).astype(o_ref.dtype)


def _rmsnorm(x, w, out_dtype, t):
    m, d = x.shape
    return pl.pallas_call(
        _rmsnorm_kernel,
        grid=(m // t,),
        in_specs=[pl.BlockSpec((t, d), lambda i: (i, 0)),
                  pl.BlockSpec((1, d), lambda i: (0, 0))],
        out_specs=pl.BlockSpec((t, d), lambda i: (i, 0)),
        out_shape=jax.ShapeDtypeStruct((m, d), out_dtype),
        compiler_params=_params("parallel"),
        name="rmsnorm",
    )(x, w.reshape(1, d))


def _lane_partial_sumsq(x):
    sq = x * x
    acc = sq[:, :LANES]
    for c in range(1, x.shape[1] // LANES):
        acc = acc + sq[:, c * LANES:(c + 1) * LANES]
    return acc


def _prep_kernel(x_ref, xb_ref, ssq_ref):
    x = x_ref[...]
    xb_ref[...] = x.astype(xb_ref.dtype)
    ssq_ref[...] = _lane_partial_sumsq(x)


def _prep(x, t):
    m, d = x.shape
    return pl.pallas_call(
        _prep_kernel,
        grid=(m // t,),
        in_specs=[pl.BlockSpec((t, d), lambda i: (i, 0))],
        out_specs=[pl.BlockSpec((t, d), lambda i: (i, 0)), pl.BlockSpec((t, LANES), lambda i: (i, 0))],
        out_shape=[jax.ShapeDtypeStruct((m, d), BF16), jax.ShapeDtypeStruct((m, LANES), F32)],
        compiler_params=_params("parallel"),
        name="prep",
    )(x)


def _weight_copy(w_hbm, wbuf_ref, sem_ref, e, tile, slot):
    tn = wbuf_ref.shape[2]
    cols = pl.ds(pl.multiple_of(tile * tn, tn), tn)
    return pltpu.make_async_copy(w_hbm.at[e, :, cols], wbuf_ref.at[slot], sem_ref.at[slot])


def _stage_weight(w_hbm, wbuf_ref, sem_ref, wbf_ref, e, tile0, row_scale_ref=None):
    j, nj = pl.program_id(0), pl.num_programs(0)

    @pl.when(pl.program_id(1) == 0)
    def _():
        slot = j % 2

        @pl.when(j == 0)
        def _():
            _weight_copy(w_hbm, wbuf_ref, sem_ref, e, tile0, 0).start()

        @pl.when(j + 1 < nj)
        def _():
            _weight_copy(w_hbm, wbuf_ref, sem_ref, e, tile0 + j + 1, 1 - slot).start()

        _weight_copy(w_hbm, wbuf_ref, sem_ref, e, tile0 + j, slot).wait()
        w = wbuf_ref[slot]
        wbf_ref[...] = (w if row_scale_ref is None else w * row_scale_ref[...]).astype(BF16)


def _weight_scratch(k, tn):
    return [pltpu.VMEM((2, k, tn), F32), pltpu.SemaphoreType.DMA((2,)), pltpu.VMEM((k, tn), BF16)]


W_HBM_SPEC = pl.BlockSpec(memory_space=pl.ANY)


def _norm_rope(z, w, cos, sin_lo, sin_hi):
    ms = jnp.mean(z * z, axis=-1, keepdims=True)
    y = z * lax.rsqrt(ms + NORM_EPS) * w
    quarter = LANES // 4
    return (y * cos + pltpu.roll(y, LANES - quarter, 1) * sin_lo
            + pltpu.roll(y, quarter, 1) * sin_hi)


def _rinv(ssq_ref, d):
    return lax.rsqrt(jnp.sum(ssq_ref[...], axis=-1, keepdims=True) * (1.0 / d) + NORM_EPS)


def _row_blocks(tm):
    step = tm // PROJ_SPLIT
    return [slice(r * step, (r + 1) * step) for r in range(PROJ_SPLIT)]


def _rope_proj_kernel(h_ref, ssq_ref, lnw_ref, w_hbm, cos_ref, slo_ref, shi_ref, qn_ref, kn_ref, o_ref,
                      wbuf_ref, sem_ref, wbf_ref, *, e, q_tiles, q_scale):
    _stage_weight(w_hbm, wbuf_ref, sem_ref, wbf_ref, e, 0, lnw_ref)
    nw = jnp.where(pl.program_id(0) < q_tiles, qn_ref[...] * q_scale, kn_ref[...])
    rinv = _rinv(ssq_ref, h_ref.shape[1])
    for rs in _row_blocks(h_ref.shape[0]):
        acc = jnp.dot(h_ref[rs, :], wbf_ref[...], preferred_element_type=F32) * rinv[rs, :]
        cos, slo, shi = cos_ref[rs, :], slo_ref[rs, :], shi_ref[rs, :]
        for hh in range(acc.shape[1] // LANES):
            sl = slice(hh * LANES, (hh + 1) * LANES)
            o_ref[rs, sl] = _norm_rope(acc[:, sl], nw, cos, slo, shi).astype(o_ref.dtype)


def _rope_proj(h, ssq, lnw, w, e, n_q, n_k, q_scale, rope, qn, kn, cfg):
    m, k = h.shape
    tm, tn = min(cfg.tm, m), cfg.tn
    row = lambda j, i: (i, 0)
    vec = pl.BlockSpec((1, LANES), lambda j, i: (0, 0))
    return pl.pallas_call(
        functools.partial(_rope_proj_kernel, e=e, q_tiles=n_q // tn, q_scale=q_scale),
        grid=((n_q + n_k) // tn, m // tm),
        in_specs=[pl.BlockSpec((tm, k), row), pl.BlockSpec((tm, LANES), row),
                  pl.BlockSpec((k, 1), lambda j, i: (0, 0)), W_HBM_SPEC,
                  pl.BlockSpec((tm, LANES), row), pl.BlockSpec((tm, LANES), row),
                  pl.BlockSpec((tm, LANES), row), vec, vec],
        out_specs=pl.BlockSpec((tm, tn), lambda j, i: (i, j)),
        out_shape=jax.ShapeDtypeStruct((m, n_q + n_k), BF16),
        scratch_shapes=_weight_scratch(k, tn),
        compiler_params=_params("arbitrary", "arbitrary"),
        name="ropeproj",
    )(h, ssq, lnw.reshape(k, 1), w, *rope, qn.reshape(1, LANES), kn.reshape(1, LANES))


def _cast_proj_kernel(scale_ref, h_ref, ssq_ref, lnw_ref, w_hbm, o_ref, wbuf_ref, sem_ref, wbf_ref, *,
                      e, tile0):
    _stage_weight(w_hbm, wbuf_ref, sem_ref, wbf_ref, e, tile0, lnw_ref)
    scale = _rinv(ssq_ref, h_ref.shape[1]) * scale_ref[pl.program_id(0)]
    for rs in _row_blocks(h_ref.shape[0]):
        acc = jnp.dot(h_ref[rs, :], wbf_ref[...], preferred_element_type=F32)
        o_ref[rs, :] = (acc * scale[rs, :]).astype(o_ref.dtype)


def _cast_proj(h, ssq, lnw, w, e, col0, widths_scales, cfg):
    m, k = h.shape
    tm, tn = min(cfg.tm, m), cfg.tn
    scales = np.concatenate([np.full(width // tn, scale, np.float32) for width, scale in widths_scales])
    n_out = tn * len(scales)
    return pl.pallas_call(
        functools.partial(_cast_proj_kernel, e=e, tile0=col0 // tn),
        grid=(n_out // tn, m // tm),
        in_specs=[pl.BlockSpec(memory_space=pltpu.SMEM),
                  pl.BlockSpec((tm, k), lambda j, i: (i, 0)),
                  pl.BlockSpec((tm, LANES), lambda j, i: (i, 0)),
                  pl.BlockSpec((k, 1), lambda j, i: (0, 0)), W_HBM_SPEC],
        out_specs=pl.BlockSpec((tm, tn), lambda j, i: (i, j)),
        out_shape=jax.ShapeDtypeStruct((m, n_out), BF16),
        scratch_shapes=_weight_scratch(k, tn),
        compiler_params=_params("arbitrary", "arbitrary"),
        name="castproj",
    )(jnp.asarray(scales), h, ssq, lnw.reshape(k, 1), w)


def _gate_kernel(h_ref, ssq_ref, lnw_ref, w_hbm, o_ref, wbuf_ref, sem_ref, wbf_ref, *, e, tile0):
    _stage_weight(w_hbm, wbuf_ref, sem_ref, wbf_ref, e, tile0, lnw_ref)
    rinv = _rinv(ssq_ref, h_ref.shape[1])
    for rs in _row_blocks(h_ref.shape[0]):
        z = jnp.dot(h_ref[rs, :], wbf_ref[...], preferred_element_type=F32) * rinv[rs, :]
        o_ref[rs, :] = (z * (0.5 * jnp.tanh(0.5 * z) + 0.5)).astype(o_ref.dtype)


def _gate_proj(h, ssq, lnw, w, e, col0, n_out, cfg):
    m, k = h.shape
    tm, tn = min(cfg.tm, m), cfg.tn
    return pl.pallas_call(
        functools.partial(_gate_kernel, e=e, tile0=col0 // tn),
        grid=(n_out // tn, m // tm),
        in_specs=[pl.BlockSpec((tm, k), lambda j, i: (i, 0)),
                  pl.BlockSpec((tm, LANES), lambda j, i: (i, 0)),
                  pl.BlockSpec((k, 1), lambda j, i: (0, 0)), W_HBM_SPEC],
        out_specs=pl.BlockSpec((tm, tn), lambda j, i: (i, j)),
        out_shape=jax.ShapeDtypeStruct((m, n_out), BF16),
        scratch_shapes=_weight_scratch(k, tn),
        compiler_params=_params("arbitrary", "arbitrary"),
        name="gateproj",
    )(h, ssq, lnw.reshape(k, 1), w)


def _outproj_kernel(*refs, n_lhs, e):
    y_refs = refs[:n_lhs]
    w_hbm, x_ref, o_ref, ob_ref, ssq_hbm, wbuf_ref, sem_ref, wbf_ref, ssq_acc_ref, ssq_sem = refs[n_lhs:]
    _stage_weight(w_hbm, wbuf_ref, sem_ref, wbf_ref, e, 0)
    acc = x_ref[...]
    k0 = 0
    for y_ref in y_refs:
        kk = y_ref.shape[1]
        acc = acc + jnp.dot(y_ref[...], wbf_ref[k0:k0 + kk, :], preferred_element_type=F32)
        k0 += kk
    o_ref[...] = acc
    ob_ref[...] = acc.astype(ob_ref.dtype)
    tm = acc.shape[0]
    rows = pl.ds(pl.multiple_of(pl.program_id(1) * tm, tm), tm)
    part = _lane_partial_sumsq(acc)

    @pl.when(pl.program_id(0) == 0)
    def _():
        ssq_acc_ref[rows, :] = part

    @pl.when(pl.program_id(0) > 0)
    def _():
        ssq_acc_ref[rows, :] += part

    @pl.when(pl.program_id(0) == pl.num_programs(0) - 1)
    def _():
        copy = pltpu.make_async_copy(ssq_acc_ref.at[rows, :], ssq_hbm.at[rows, :], ssq_sem.at[0])
        copy.start()
        copy.wait()


def _outproj(ys, w, e, x, cfg):
    m, n = x.shape
    k = w.shape[1]
    tm, tn = min(cfg.tm, m), cfg.tn
    in_specs = [pl.BlockSpec((tm, y.shape[1]), lambda j, i: (i, 0)) for y in ys]
    in_specs += [W_HBM_SPEC, pl.BlockSpec((tm, tn), lambda j, i: (i, j))]
    return pl.pallas_call(
        functools.partial(_outproj_kernel, n_lhs=len(ys), e=e),
        grid=(n // tn, m // tm),
        in_specs=in_specs,
        out_specs=[pl.BlockSpec((tm, tn), lambda j, i: (i, j)),
                   pl.BlockSpec((tm, tn), lambda j, i: (i, j)),
                   pl.BlockSpec(memory_space=pl.ANY)],
        out_shape=[jax.ShapeDtypeStruct((m, n), F32), jax.ShapeDtypeStruct((m, n), BF16),
                   jax.ShapeDtypeStruct((m, LANES), F32)],
        scratch_shapes=_weight_scratch(k, tn) + [pltpu.VMEM((m, LANES), F32),
                                                 pltpu.SemaphoreType.DMA((1,))],
        compiler_params=_params("arbitrary", "arbitrary"),
        name="outproj",
    )(*ys, w, x)


def _attn_a_kernel(q_ref, k_ref, v_ref, g_ref, o_ref, qs_ref, v1_ref, s_ref, p_ref, m_ref, alpha_ref,
                   acc_ref, kmax_ref, *, tk, group, chunk):
    tq = q_ref.shape[0]
    s_len = k_ref.shape[0]
    nk = s_len // tk

    @pl.when(pl.program_id(1) == 0)
    def _():
        v1_ref[:, :LANES] = v_ref[...]
        v1_ref[:, LANES:] = jnp.ones((s_len, LANES), BF16)
        kf = k_ref[...].astype(F32)
        kmax_ref[0] = jnp.max(jnp.sum(kf * kf, axis=-1, keepdims=True))

    for g in range(group):
        qs_ref[g * tq:(g + 1) * tq, :] = q_ref[:, g * LANES:(g + 1) * LANES]
    rows = group * tq

    def scores(t):
        k = k_ref[t * tk:(t + 1) * tk, :]
        return lax.dot_general(qs_ref[...], k, (((1,), (1,)), ((), ())), preferred_element_type=F32)

    qf = qs_ref[...].astype(F32)
    bound = jnp.sqrt(jnp.sum(qf * qf, axis=-1, keepdims=True) * kmax_ref[0])

    def fixed_shift():
        acc_ref[...] = jnp.zeros((rows, 2 * LANES), F32)
        for t in range(nk):
            p = jnp.exp2(scores(t) - bound).astype(BF16)
            acc_ref[...] += jnp.dot(p, v1_ref[t * tk:(t + 1) * tk, :], preferred_element_type=F32)

    def running_max():
        m_ref[...] = jnp.full((rows, 1), NEG_INF, F32)
        acc_ref[...] = jnp.zeros((rows, 2 * LANES), F32)
        s_ref[0] = scores(0)
        for t in range(nk):
            if t + 1 < nk:
                s_ref[(t + 1) % 2] = scores(t + 1)
            for c in range(rows // chunk):
                rs = slice(c * chunk, (c + 1) * chunk)
                s = s_ref[t % 2, rs, :]
                m_old = m_ref[rs, :]
                m_new = jnp.maximum(m_old, jnp.max(s, axis=-1, keepdims=True))
                m_ref[rs, :] = m_new
                alpha_ref[rs, :] = jnp.exp2(m_old - m_new)
                p_ref[t % 2, rs, :] = jnp.exp2(s - m_new).astype(BF16)
            pv = jnp.dot(p_ref[t % 2], v1_ref[t * tk:(t + 1) * tk, :], preferred_element_type=F32)
            acc_ref[...] = alpha_ref[...] * acc_ref[...] + pv

    lax.cond(jnp.max(bound) <= A_SAFE_SHIFT, fixed_shift, running_max)
    o = acc_ref[:, :LANES] / acc_ref[:, LANES:]
    for g in range(group):
        sl = slice(g * LANES, (g + 1) * LANES)
        o_ref[:, sl] = (o[g * tq:(g + 1) * tq, :] * g_ref[:, sl]).astype(o_ref.dtype)


def _attn_a(q, k, v, gate, cfg):
    (q_arr, q_col0), (k_arr, k_col0), (v_arr, v_col0) = q, k, v
    s_len = q_arr.shape[0]
    group = cfg.a_heads // cfg.a_kv_heads
    gw = group * LANES
    tq = min(cfg.tq_a, s_len)
    tk = min(cfg.tk_a, s_len)
    return pl.pallas_call(
        functools.partial(_attn_a_kernel, tk=tk, group=group, chunk=cfg.chunk_a),
        grid=(cfg.a_kv_heads, s_len // tq),
        scratch_shapes=[pltpu.VMEM((group * tq, LANES), BF16),
                        pltpu.VMEM((s_len, 2 * LANES), BF16),
                        pltpu.VMEM((2, group * tq, tk), F32),
                        pltpu.VMEM((2, group * tq, tk), BF16),
                        pltpu.VMEM((group * tq, 1), F32),
                        pltpu.VMEM((group * tq, 1), F32),
                        pltpu.VMEM((group * tq, 2 * LANES), F32),
                        pltpu.SMEM((1,), F32)],
        in_specs=[pl.BlockSpec((tq, gw), lambda kv, i: (i, q_col0 // gw + kv)),
                  pl.BlockSpec((s_len, LANES), lambda kv, i: (0, k_col0 // LANES + kv)),
                  pl.BlockSpec((s_len, LANES), lambda kv, i: (0, v_col0 // LANES + kv)),
                  pl.BlockSpec((tq, gw), lambda kv, i: (i, kv))],
        out_specs=pl.BlockSpec((tq, gw), lambda kv, i: (i, kv)),
        out_shape=jax.ShapeDtypeStruct((s_len, cfg.a_heads * LANES), BF16),
        compiler_params=_params("parallel", "arbitrary"),
        name="attn_global",
    )(q_arr, k_arr, v_arr, gate)


def _natten_geometry(rows):
    types, block_type = [], []
    for b in range(rows // NA_QB_ROWS):
        r = b * NA_QB_ROWS + np.arange(NA_QB_ROWS)
        u0 = int(np.clip(r[0] - NA_ROWS // 2, 0, rows - NA_UNION_ROWS))
        rs = np.clip(r - NA_ROWS // 2, 0, rows - NA_ROWS)
        assert u0 <= rs.min() and rs.max() + NA_ROWS <= u0 + NA_UNION_ROWS
        kr = u0 + np.arange(NA_UNION_ROWS)
        vr = (kr[None, :] >= rs[:, None]) & (kr[None, :] < rs[:, None] + NA_ROWS)
        dr = np.where(vr, kr[None, :] - r[:, None] + (NA_ROWS - 1), 0)
        for t, (dr_t, vr_t) in enumerate(types):
            if np.array_equal(dr, dr_t) and np.array_equal(vr, vr_t):
                block_type.append(t)
                break
        else:
            block_type.append(len(types))
            types.append((dr, vr))
    return types, block_type


def _build_natten_tables(rpb_ref, tbl_ref, types):
    c = lax.broadcasted_iota(jnp.int32, (GRID_W, LANES), 0)
    lane = lax.broadcasted_iota(jnp.int32, (GRID_W, LANES), 1)
    first = lane < GRID_W
    kc = jnp.where(first, lane, lane - GRID_W)
    cs = jnp.clip(c - NA_COLS // 2, 0, GRID_W - NA_COLS)
    vc = (kc >= cs) & (kc < cs + NA_COLS)
    base_shift = LANES - (NA_COLS - 1)
    for t, (dr, vr) in enumerate(types):
        for i in range(NA_QB_ROWS):
            for kp in range(NA_UNION_ROWS // 2):
                halves = []
                for half in range(2):
                    ku = 2 * kp + half
                    if vr[i, ku]:
                        row = jnp.broadcast_to(rpb_ref[int(dr[i, ku]):int(dr[i, ku]) + 1, :],
                                               (GRID_W, LANES))
                        halves.append(pltpu.roll(row, (base_shift + half * GRID_W) % LANES, 1,
                                                 stride=1, stride_axis=0))
                    else:
                        halves.append(None)
                lo, hi = halves
                if lo is None and hi is None:
                    tile = jnp.full((GRID_W, LANES), NEG_INF, F32)
                else:
                    ok = vc
                    if lo is None:
                        val, ok = hi, vc & ~first
                    elif hi is None:
                        val, ok = lo, vc & first
                    else:
                        val = jnp.where(first, lo, hi)
                    tile = jnp.where(ok, val * LOG2E, NEG_INF)
                tbl_ref[t, i * GRID_W:(i + 1) * GRID_W, kp * LANES:(kp + 1) * LANES] = tile


def _natten_kernel(q_ref, k_ref, v_ref, g_ref, rpb_ref, o_ref, tbl_ref, v1_ref, *, rows):
    qb = NA_QB_ROWS * GRID_W
    span = NA_UNION_ROWS * GRID_W
    nb_step = q_ref.shape[0] // qb
    types, block_type = _natten_geometry(rows)
    common = max(set(block_type), key=block_type.count)
    step = pl.program_id(1)

    @pl.when(step == 0)
    def _():
        _build_natten_tables(rpb_ref, tbl_ref, types)
        v1_ref[:, :LANES] = v_ref[...]
        v1_ref[:, LANES:] = jnp.ones((v_ref.shape[0], LANES), BF16)

    for b in range(nb_step):
        blk = step * nb_step + b
        u0 = jnp.clip(blk * NA_QB_ROWS - NA_ROWS // 2, 0, rows - NA_UNION_ROWS)
        typ = common
        for bb, t in enumerate(block_type):
            if t != common:
                typ = jnp.where(blk == bb, t, typ)
        koff = pl.multiple_of(u0 * GRID_W, GRID_W)
        qs = slice(b * qb, (b + 1) * qb)
        k = k_ref[pl.ds(koff, span), :]
        v1 = v1_ref[pl.ds(koff, span), :]
        s = lax.dot_general(q_ref[qs, :], k, (((1,), (1,)), ((), ())), preferred_element_type=F32)
        s = s + tbl_ref[typ]
        m = jnp.max(s, axis=-1, keepdims=True)
        pv = jnp.dot(jnp.exp2(s - m).astype(BF16), v1, preferred_element_type=F32)
        o_ref[qs, :] = (pv[:, :LANES] / pv[:, LANES:] * g_ref[qs, :]).astype(o_ref.dtype)


def _natten(q, k, v, gate, gate_col0, rpb, cfg):
    (q_arr, q_col0), (k_arr, k_col0), (v_arr, v_col0) = q, k, v
    s_len = q_arr.shape[0]
    rows = s_len // GRID_W
    tq = min(cfg.tq_b, s_len)
    qb = NA_QB_ROWS * GRID_W
    span = NA_UNION_ROWS * GRID_W
    assert rows % NA_QB_ROWS == 0 and rows >= NA_UNION_ROWS and tq % qb == 0
    n_types = len(_natten_geometry(rows)[0])
    nh, nr, nc = rpb.shape
    rpb_pad = jnp.pad(rpb.astype(F32), ((0, 0), (0, 2 * NA_ROWS - nr), (0, LANES - nc)))
    return pl.pallas_call(
        functools.partial(_natten_kernel, rows=rows),
        grid=(cfg.b_heads, s_len // tq),
        in_specs=[pl.BlockSpec((tq, LANES), lambda h, i: (i, q_col0 // LANES + h)),
                  pl.BlockSpec((s_len, LANES), lambda h, i: (0, k_col0 // LANES + h)),
                  pl.BlockSpec((s_len, LANES), lambda h, i: (0, v_col0 // LANES + h)),
                  pl.BlockSpec((tq, LANES), lambda h, i: (i, gate_col0 // LANES + h)),
                  pl.BlockSpec((None, 2 * NA_ROWS, LANES), lambda h, i: (h, 0, 0))],
        out_specs=pl.BlockSpec((tq, LANES), lambda h, i: (i, h)),
        out_shape=jax.ShapeDtypeStruct((s_len, cfg.b_heads * LANES), BF16),
        scratch_shapes=[pltpu.VMEM((n_types, qb, span), F32),
                        pltpu.VMEM((s_len, 2 * LANES), BF16)],
        compiler_params=_params("parallel", "arbitrary"),
        name="attn_neighbourhood",
    )(q_arr, k_arr, v_arr, gate, rpb_pad)


def _attn_c_kernel(slope_ref, sink_ref, q_ref, k_ref, v_ref, g_ref, o_ref, v1_ref, *, group, sub):
    tq = q_ref.shape[0]
    s_len = k_ref.shape[0]
    span = sub + 2 * C_WINDOW
    kv = pl.program_id(0)

    @pl.when(pl.program_id(1) == 0)
    def _():
        v1_ref[:, :LANES] = v_ref[...]
        v1_ref[:, LANES:] = jnp.ones((s_len, LANES), BF16)

    slopes = [slope_ref[kv * group + g] * LOG2E for g in range(group)]
    sinks = [sink_ref[kv * group + g] * LOG2E for g in range(group)]
    row = lax.broadcasted_iota(jnp.int32, (sub, span), 0)
    col = lax.broadcasted_iota(jnp.int32, (sub, span), 1)
    for b in range(tq // sub):
        t0 = pl.program_id(1) * tq + b * sub
        start = pl.multiple_of(jnp.clip(t0 - C_WINDOW, 0, s_len - span), C_WINDOW)
        k = k_ref[pl.ds(start, span), :]
        v1 = v1_ref[pl.ds(start, span), :]
        rows = slice(b * sub, (b + 1) * sub)
        q = jnp.concatenate([q_ref[rows, g * LANES:(g + 1) * LANES] for g in range(group)], axis=0)
        s = lax.dot_general(q, k, (((1,), (1,)), ((), ())), preferred_element_type=F32)
        dist = jnp.abs(row - col + (t0 - start))
        pen = jnp.where(dist <= C_WINDOW, dist.astype(F32), -NEG_INF)
        ps, ms = [], []
        for g in range(group):
            sg = s[g * sub:(g + 1) * sub, :] - slopes[g] * pen
            m = jnp.maximum(jnp.max(sg, axis=-1, keepdims=True), sinks[g])
            ps.append(jnp.exp2(sg - m).astype(BF16))
            ms.append(m)
        pv = jnp.dot(jnp.concatenate(ps, axis=0), v1, preferred_element_type=F32)
        for g in range(group):
            sl = slice(g * LANES, (g + 1) * LANES)
            pg = pv[g * sub:(g + 1) * sub, :]
            l = pg[:, LANES:] + jnp.exp2(sinks[g] - ms[g])
            o_ref[rows, sl] = (pg[:, :LANES] / l * g_ref[rows, sl]).astype(o_ref.dtype)


def _attn_c(q, k, v, gate, slopes, sink, cfg):
    (q_arr, q_col0), (k_arr, k_col0), (v_arr, v_col0) = q, k, v
    s_len = q_arr.shape[0]
    group = cfg.c_heads // cfg.c_kv_heads
    gw = group * LANES
    tq, sub = cfg.tq_c, cfg.sub_c
    assert s_len >= sub + 2 * C_WINDOW and s_len % tq == 0 and tq % sub == 0 and sub % C_WINDOW == 0
    smem = pl.BlockSpec(memory_space=pltpu.SMEM)
    return pl.pallas_call(
        functools.partial(_attn_c_kernel, group=group, sub=sub),
        grid=(cfg.c_kv_heads, s_len // tq),
        scratch_shapes=[pltpu.VMEM((s_len, 2 * LANES), BF16)],
        in_specs=[smem, smem,
                  pl.BlockSpec((tq, gw), lambda kv, i: (i, q_col0 // gw + kv)),
                  pl.BlockSpec((s_len, LANES), lambda kv, i: (0, k_col0 // LANES + kv)),
                  pl.BlockSpec((s_len, LANES), lambda kv, i: (0, v_col0 // LANES + kv)),
                  pl.BlockSpec((tq, gw), lambda kv, i: (i, kv))],
        out_specs=pl.BlockSpec((tq, gw), lambda kv, i: (i, kv)),
        out_shape=jax.ShapeDtypeStruct((s_len, cfg.c_heads * LANES), BF16),
        compiler_params=_params("parallel", "arbitrary"),
        name="attn_window",
    )(slopes, sink, q_arr, k_arr, v_arr, gate)


def _rope_tables(s_len):
    half = LANES // 2
    quarter = half // 2
    inv = jnp.exp(-math.log(ROPE_THETA) * jnp.arange(0, half, 2, dtype=F32) / half)
    pos = jnp.arange(s_len, dtype=jnp.int32)
    rows = (pos // GRID_W).astype(F32)
    cols = (pos % GRID_W).astype(F32)
    ang = jnp.concatenate([rows[:, None] * inv[None, :]] * 2 + [cols[:, None] * inv[None, :]] * 2, axis=1)
    first = (np.arange(LANES) % half) < quarter
    cos, sin = jnp.cos(ang), jnp.sin(ang)
    return cos, jnp.where(first[None], -sin, 0.0), jnp.where(first[None], 0.0, sin)


def _forward(x, norm_w, w_in_ab, w_out_ab, q_norm_a, k_norm_a, rpb_b, w_in_c, w_out_c, sink_c,
             final_norm_w, cfg):
    bsz, s_len, d = x.shape
    assert bsz == 1 and s_len % GRID_W == 0
    depth = norm_w.shape[0]
    scale = LANES ** -0.5 * LOG2E
    a_q, a_kv, b_w = cfg.a_heads * LANES, cfg.a_kv_heads * LANES, cfg.b_heads * LANES
    c_q, c_kv = cfg.c_heads * LANES, cfg.c_kv_heads * LANES
    rope = _rope_tables(s_len)
    slopes = jnp.exp2(-8.0 * jnp.arange(1, cfg.c_heads + 1, dtype=F32) / cfg.c_heads)
    xs = x.reshape(s_len, d)
    h, ssq = _prep(xs, cfg.t_norm)
    for layer in range(depth):
        lnw = norm_w[layer]
        if layer % 2 == 0:
            e = layer // 2
            n_rope = a_q + a_kv
            zr = _rope_proj(h, ssq, lnw, w_in_ab, e, a_q, a_kv, scale, rope, q_norm_a[e], k_norm_a[e], cfg)
            zc = _cast_proj(h, ssq, lnw, w_in_ab, e, n_rope, ((a_kv, 1.0), (b_w, scale), (2 * b_w, 1.0)),
                            cfg)
            gate = _gate_proj(h, ssq, lnw, w_in_ab, e, n_rope + a_kv + 3 * b_w, a_q + b_w, cfg)
            ya = _attn_a((zr, 0), (zr, a_q), (zc, 0), gate, cfg)
            yb = _natten((zc, a_kv), (zc, a_kv + b_w), (zc, a_kv + 2 * b_w), gate, a_q, rpb_b[e], cfg)
            xs, h, ssq = _outproj([ya, yb], w_out_ab, e, xs, cfg)
        else:
            o = layer // 2
            zc = _cast_proj(h, ssq, lnw, w_in_c, o, 0, ((c_q, scale), (2 * c_kv, 1.0)), cfg)
            gate = _gate_proj(h, ssq, lnw, w_in_c, o, c_q + 2 * c_kv, c_q, cfg)
            y = _attn_c((zc, 0), (zc, c_q), (zc, c_q + c_kv), gate, slopes, sink_c[o].astype(F32), cfg)
            xs, h, ssq = _outproj([y], w_out_c, o, xs, cfg)
    out = _rmsnorm(xs, final_norm_w, x.dtype, cfg.t_norm)
    return out.reshape(bsz, s_len, d)


def kernel(x, norm_w, w_in_ab, w_out_ab, q_norm_a, k_norm_a, rpb_b, w_in_c, w_out_c, sink_c, final_norm_w):
    return _forward(x, norm_w, w_in_ab, w_out_ab, q_norm_a, k_norm_a, rpb_b, w_in_c, w_out_c, sink_c,
                    final_norm_w, Config())
```

```python
import functools
import math
from typing import NamedTuple

import jax
import jax.numpy as jnp
import numpy as np
from jax import lax
from jax.experimental import pallas as pl
from jax.experimental.pallas import tpu as pltpu

F32 = jnp.float32
BF16 = jnp.bfloat16

LANES = 128
NORM_EPS = 1e-6
NEG_INF = -1e30
LOG2E = math.log2(math.e)
ROPE_THETA = 10000.0
GRID_W = 64
NA_ROWS = 8
NA_COLS = 16
NA_QB_ROWS = 4
NA_UNION_ROWS = 12
C_WINDOW = 128
PROJ_SPLIT = 8
A_SAFE_SHIFT = 60.0
VMEM_LIMIT_BYTES = 56 * 1024 * 1024


class Config(NamedTuple):
    a_heads: int = 16
    a_kv_heads: int = 4
    b_heads: int = 16
    c_heads: int = 32
    c_kv_heads: int = 8
    tm: int = 1024
    tn: int = 512
    tq_a: int = 512
    tk_a: int = 1024
    chunk_a: int = 16
    tq_c: int = 512
    sub_c: int = 128
    tq_b: int = 2048
    t_norm: int = 256


def _params(*sem):
    return pltpu.CompilerParams(dimension_semantics=sem, vmem_limit_bytes=VMEM_LIMIT_BYTES)


def _rmsnorm_kernel(x_ref, w_ref, o_ref):
    x = x_ref[...]
    ms = jnp.mean(x * x, axis=-1, keepdims=True)
    o_ref[...] = (x * lax.rsqrt(ms + NORM_EPS) * w_ref[...]).astype(o_ref.dtype)


def _rmsnorm(x, w, out_dtype, t):
    m, d = x.shape
    return pl.pallas_call(
        _rmsnorm_kernel,
        grid=(m // t,),
        in_specs=[pl.BlockSpec((t, d), lambda i: (i, 0)),
                  pl.BlockSpec((1, d), lambda i: (0, 0))],
        out_specs=pl.BlockSpec((t, d), lambda i: (i, 0)),
        out_shape=jax.ShapeDtypeStruct((m, d), out_dtype),
        compiler_params=_params("parallel"),
        name="rmsnorm",
    )(x, w.reshape(1, d))


def _lane_partial_sumsq(x):
    sq = x * x
    acc = sq[:, :LANES]
    for c in range(1, x.shape[1] // LANES):
        acc = acc + sq[:, c * LANES:(c + 1) * LANES]
    return acc


def _prep_kernel(x_ref, xb_ref, ssq_ref):
    x = x_ref[...]
    xb_ref[...] = x.astype(xb_ref.dtype)
    ssq_ref[...] = _lane_partial_sumsq(x)


def _prep(x, t):
    m, d = x.shape
    return pl.pallas_call(
        _prep_kernel,
        grid=(m // t,),
        in_specs=[pl.BlockSpec((t, d), lambda i: (i, 0))],
        out_specs=[pl.BlockSpec((t, d), lambda i: (i, 0)), pl.BlockSpec((t, LANES), lambda i: (i, 0))],
        out_shape=[jax.ShapeDtypeStruct((m, d), BF16), jax.ShapeDtypeStruct((m, LANES), F32)],
        compiler_params=_params("parallel"),
        name="prep",
    )(x)


def _weight_copy(w_hbm, wbuf_ref, sem_ref, e, tile, slot):
    tn = wbuf_ref.shape[2]
    cols = pl.ds(pl.multiple_of(tile * tn, tn), tn)
    return pltpu.make_async_copy(w_hbm.at[e, :, cols], wbuf_ref.at[slot], sem_ref.at[slot])


def _stage_weight(w_hbm, wbuf_ref, sem_ref, wbf_ref, e, tile0, row_scale_ref=None):
    j, nj = pl.program_id(0), pl.num_programs(0)

    @pl.when(pl.program_id(1) == 0)
    def _():
        slot = j % 2

        @pl.when(j == 0)
        def _():
            _weight_copy(w_hbm, wbuf_ref, sem_ref, e, tile0, 0).start()

        @pl.when(j + 1 < nj)
        def _():
            _weight_copy(w_hbm, wbuf_ref, sem_ref, e, tile0 + j + 1, 1 - slot).start()

        _weight_copy(w_hbm, wbuf_ref, sem_ref, e, tile0 + j, slot).wait()
        w = wbuf_ref[slot]
        wbf_ref[...] = (w if row_scale_ref is None else w * row_scale_ref[...]).astype(BF16)


def _weight_scratch(k, tn):
    return [pltpu.VMEM((2, k, tn), F32), pltpu.SemaphoreType.DMA((2,)), pltpu.VMEM((k, tn), BF16)]


W_HBM_SPEC = pl.BlockSpec(memory_space=pl.ANY)


def _norm_rope(z, w, cos, sin_lo, sin_hi):
    ms = jnp.mean(z * z, axis=-1, keepdims=True)
    y = z * lax.rsqrt(ms + NORM_EPS) * w
    quarter = LANES // 4
    return (y * cos + pltpu.roll(y, LANES - quarter, 1) * sin_lo
            + pltpu.roll(y, quarter, 1) * sin_hi)


def _rinv(ssq_ref, d):
    return lax.rsqrt(jnp.sum(ssq_ref[...], axis=-1, keepdims=True) * (1.0 / d) + NORM_EPS)


def _row_blocks(tm):
    step = tm // PROJ_SPLIT
    return [slice(r * step, (r + 1) * step) for r in range(PROJ_SPLIT)]


def _rope_proj_kernel(h_ref, ssq_ref, lnw_ref, w_hbm, cos_ref, slo_ref, shi_ref, qn_ref, kn_ref, o_ref,
                      wbuf_ref, sem_ref, wbf_ref, *, e, q_tiles, q_scale):
    _stage_weight(w_hbm, wbuf_ref, sem_ref, wbf_ref, e, 0, lnw_ref)
    nw = jnp.where(pl.program_id(0) < q_tiles, qn_ref[...] * q_scale, kn_ref[...])
    rinv = _rinv(ssq_ref, h_ref.shape[1])
    for rs in _row_blocks(h_ref.shape[0]):
        acc = jnp.dot(h_ref[rs, :], wbf_ref[...], preferred_element_type=F32) * rinv[rs, :]
        cos, slo, shi = cos_ref[rs, :], slo_ref[rs, :], shi_ref[rs, :]
        for hh in range(acc.shape[1] // LANES):
            sl = slice(hh * LANES, (hh + 1) * LANES)
            o_ref[rs, sl] = _norm_rope(acc[:, sl], nw, cos, slo, shi).astype(o_ref.dtype)


def _rope_proj(h, ssq, lnw, w, e, n_q, n_k, q_scale, rope, qn, kn, cfg):
    m, k = h.shape
    tm, tn = min(cfg.tm, m), cfg.tn
    row = lambda j, i: (i, 0)
    vec = pl.BlockSpec((1, LANES), lambda j, i: (0, 0))
    return pl.pallas_call(
        functools.partial(_rope_proj_kernel, e=e, q_tiles=n_q // tn, q_scale=q_scale),
        grid=((n_q + n_k) // tn, m // tm),
        in_specs=[pl.BlockSpec((tm, k), row), pl.BlockSpec((tm, LANES), row),
                  pl.BlockSpec((k, 1), lambda j, i: (0, 0)), W_HBM_SPEC,
                  pl.BlockSpec((tm, LANES), row), pl.BlockSpec((tm, LANES), row),
                  pl.BlockSpec((tm, LANES), row), vec, vec],
        out_specs=pl.BlockSpec((tm, tn), lambda j, i: (i, j)),
        out_shape=jax.ShapeDtypeStruct((m, n_q + n_k), BF16),
        scratch_shapes=_weight_scratch(k, tn),
        compiler_params=_params("arbitrary", "arbitrary"),
        name="ropeproj",
    )(h, ssq, lnw.reshape(k, 1), w, *rope, qn.reshape(1, LANES), kn.reshape(1, LANES))


def _cast_proj_kernel(scale_ref, h_ref, ssq_ref, lnw_ref, w_hbm, o_ref, wbuf_ref, sem_ref, wbf_ref, *,
                      e, tile0):
    _stage_weight(w_hbm, wbuf_ref, sem_ref, wbf_ref, e, tile0, lnw_ref)
    scale = _rinv(ssq_ref, h_ref.shape[1]) * scale_ref[pl.program_id(0)]
    for rs in _row_blocks(h_ref.shape[0]):
        acc = jnp.dot(h_ref[rs, :], wbf_ref[...], preferred_element_type=F32)
        o_ref[rs, :] = (acc * scale[rs, :]).astype(o_ref.dtype)


def _cast_proj(h, ssq, lnw, w, e, col0, widths_scales, cfg):
    m, k = h.shape
    tm, tn = min(cfg.tm, m), cfg.tn
    scales = np.concatenate([np.full(width // tn, scale, np.float32) for width, scale in widths_scales])
    n_out = tn * len(scales)
    return pl.pallas_call(
        functools.partial(_cast_proj_kernel, e=e, tile0=col0 // tn),
        grid=(n_out // tn, m // tm),
        in_specs=[pl.BlockSpec(memory_space=pltpu.SMEM),
                  pl.BlockSpec((tm, k), lambda j, i: (i, 0)),
                  pl.BlockSpec((tm, LANES), lambda j, i: (i, 0)),
                  pl.BlockSpec((k, 1), lambda j, i: (0, 0)), W_HBM_SPEC],
        out_specs=pl.BlockSpec((tm, tn), lambda j, i: (i, j)),
        out_shape=jax.ShapeDtypeStruct((m, n_out), BF16),
        scratch_shapes=_weight_scratch(k, tn),
        compiler_params=_params("arbitrary", "arbitrary"),
        name="castproj",
    )(jnp.asarray(scales), h, ssq, lnw.reshape(k, 1), w)


def _gate_kernel(h_ref, ssq_ref, lnw_ref, w_hbm, o_ref, wbuf_ref, sem_ref, wbf_ref, *, e, tile0):
    _stage_weight(w_hbm, wbuf_ref, sem_ref, wbf_ref, e, tile0, lnw_ref)
    rinv = _rinv(ssq_ref, h_ref.shape[1])
    for rs in _row_blocks(h_ref.shape[0]):
        z = jnp.dot(h_ref[rs, :], wbf_ref[...], preferred_element_type=F32) * rinv[rs, :]
        o_ref[rs, :] = (z * (0.5 * jnp.tanh(0.5 * z) + 0.5)).astype(o_ref.dtype)


def _gate_proj(h, ssq, lnw, w, e, col0, n_out, cfg):
    m, k = h.shape
    tm, tn = min(cfg.tm, m), cfg.tn
    return pl.pallas_call(
        functools.partial(_gate_kernel, e=e, tile0=col0 // tn),
        grid=(n_out // tn, m // tm),
        in_specs=[pl.BlockSpec((tm, k), lambda j, i: (i, 0)),
                  pl.BlockSpec((tm, LANES), lambda j, i: (i, 0)),
                  pl.BlockSpec((k, 1), lambda j, i: (0, 0)), W_HBM_SPEC],
        out_specs=pl.BlockSpec((tm, tn), lambda j, i: (i, j)),
        out_shape=jax.ShapeDtypeStruct((m, n_out), BF16),
        scratch_shapes=_weight_scratch(k, tn),
        compiler_params=_params("arbitrary", "arbitrary"),
        name="gateproj",
    )(h, ssq, lnw.reshape(k, 1), w)


def _outproj_kernel(*refs, n_lhs, e):
    y_refs = refs[:n_lhs]
    w_hbm, x_ref, o_ref, ob_ref, ssq_hbm, wbuf_ref, sem_ref, wbf_ref, ssq_acc_ref, ssq_sem = refs[n_lhs:]
    _stage_weight(w_hbm, wbuf_ref, sem_ref, wbf_ref, e, 0)
    acc = x_ref[...]
    k0 = 0
    for y_ref in y_refs:
        kk = y_ref.shape[1]
        acc = acc + jnp.dot(y_ref[...], wbf_ref[k0:k0 + kk, :], preferred_element_type=F32)
        k0 += kk
    o_ref[...] = acc
    ob_ref[...] = acc.astype(ob_ref.dtype)
    tm = acc.shape[0]
    rows = pl.ds(pl.multiple_of(pl.program_id(1) * tm, tm), tm)
    part = _lane_partial_sumsq(acc)

    @pl.when(pl.program_id(0) == 0)
    def _():
        ssq_acc_ref[rows, :] = part

    @pl.when(pl.program_id(0) > 0)
    def _():
        ssq_acc_ref[rows, :] += part

    @pl.when(pl.program_id(0) == pl.num_programs(0) - 1)
    def _():
        copy = pltpu.make_async_copy(ssq_acc_ref.at[rows, :], ssq_hbm.at[rows, :], ssq_sem.at[0])
        copy.start()
        copy.wait()


def _outproj(ys, w, e, x, cfg):
    m, n = x.shape
    k = w.shape[1]
    tm, tn = min(cfg.tm, m), cfg.tn
    in_specs = [pl.BlockSpec((tm, y.shape[1]), lambda j, i: (i, 0)) for y in ys]
    in_specs += [W_HBM_SPEC, pl.BlockSpec((tm, tn), lambda j, i: (i, j))]
    return pl.pallas_call(
        functools.partial(_outproj_kernel, n_lhs=len(ys), e=e),
        grid=(n // tn, m // tm),
        in_specs=in_specs,
        out_specs=[pl.BlockSpec((tm, tn), lambda j, i: (i, j)),
                   pl.BlockSpec((tm, tn), lambda j, i: (i, j)),
                   pl.BlockSpec(memory_space=pl.ANY)],
        out_shape=[jax.ShapeDtypeStruct((m, n), F32), jax.ShapeDtypeStruct((m, n), BF16),
                   jax.ShapeDtypeStruct((m, LANES), F32)],
        scratch_shapes=_weight_scratch(k, tn) + [pltpu.VMEM((m, LANES), F32),
                                                 pltpu.SemaphoreType.DMA((1,))],
        compiler_params=_params("arbitrary", "arbitrary"),
        name="outproj",
    )(*ys, w, x)


def _attn_a_kernel(q_ref, k_ref, v_ref, g_ref, o_ref, qs_ref, v1_ref, s_ref, p_ref, m_ref, alpha_ref,
                   acc_ref, kmax_ref, *, tk, group, chunk):
    tq = q_ref.shape[0]
    s_len = k_ref.shape[0]
    nk = s_len // tk

    @pl.when(pl.program_id(1) == 0)
    def _():
        v1_ref[:, :LANES] = v_ref[...]
        v1_ref[:, LANES:] = jnp.ones((s_len, LANES), BF16)
        kf = k_ref[...].astype(F32)
        kmax_ref[0] = jnp.max(jnp.sum(kf * kf, axis=-1, keepdims=True))

    for g in range(group):
        qs_ref[g * tq:(g + 1) * tq, :] = q_ref[:, g * LANES:(g + 1) * LANES]
    rows = group * tq

    def scores(t):
        k = k_ref[t * tk:(t + 1) * tk, :]
        return lax.dot_general(qs_ref[...], k, (((1,), (1,)), ((), ())), preferred_element_type=F32)

    qf = qs_ref[...].astype(F32)
    bound = jnp.sqrt(jnp.sum(qf * qf, axis=-1, keepdims=True) * kmax_ref[0])

    def fixed_shift():
        acc_ref[...] = jnp.zeros((rows, 2 * LANES), F32)
        for t in range(nk):
            p = jnp.exp2(scores(t) - bound).astype(BF16)
            acc_ref[...] += jnp.dot(p, v1_ref[t * tk:(t + 1) * tk, :], preferred_element_type=F32)

    def running_max():
        m_ref[...] = jnp.full((rows, 1), NEG_INF, F32)
        acc_ref[...] = jnp.zeros((rows, 2 * LANES), F32)
        s_ref[0] = scores(0)
        for t in range(nk):
            if t + 1 < nk:
                s_ref[(t + 1) % 2] = scores(t + 1)
            for c in range(rows // chunk):
                rs = slice(c * chunk, (c + 1) * chunk)
                s = s_ref[t % 2, rs, :]
                m_old = m_ref[rs, :]
                m_new = jnp.maximum(m_old, jnp.max(s, axis=-1, keepdims=True))
                m_ref[rs, :] = m_new
                alpha_ref[rs, :] = jnp.exp2(m_old - m_new)
                p_ref[t % 2, rs, :] = jnp.exp2(s - m_new).astype(BF16)
            pv = jnp.dot(p_ref[t % 2], v1_ref[t * tk:(t + 1) * tk, :], preferred_element_type=F32)
            acc_ref[...] = alpha_ref[...] * acc_ref[...] + pv

    lax.cond(jnp.max(bound) <= A_SAFE_SHIFT, fixed_shift, running_max)
    o = acc_ref[:, :LANES] / acc_ref[:, LANES:]
    for g in range(group):
        sl = slice(g * LANES, (g + 1) * LANES)
        o_ref[:, sl] = (o[g * tq:(g + 1) * tq, :] * g_ref[:, sl]).astype(o_ref.dtype)


def _attn_a(q, k, v, gate, cfg):
    (q_arr, q_col0), (k_arr, k_col0), (v_arr, v_col0) = q, k, v
    s_len = q_arr.shape[0]
    group = cfg.a_heads // cfg.a_kv_heads
    gw = group * LANES
    tq = min(cfg.tq_a, s_len)
    tk = min(cfg.tk_a, s_len)
    return pl.pallas_call(
        functools.partial(_attn_a_kernel, tk=tk, group=group, chunk=cfg.chunk_a),
        grid=(cfg.a_kv_heads, s_len // tq),
        scratch_shapes=[pltpu.VMEM((group * tq, LANES), BF16),
                        pltpu.VMEM((s_len, 2 * LANES), BF16),
                        pltpu.VMEM((2, group * tq, tk), F32),
                        pltpu.VMEM((2, group * tq, tk), BF16),
                        pltpu.VMEM((group * tq, 1), F32),
                        pltpu.VMEM((group * tq, 1), F32),
                        pltpu.VMEM((group * tq, 2 * LANES), F32),
                        pltpu.SMEM((1,), F32)],
        in_specs=[pl.BlockSpec((tq, gw), lambda kv, i: (i, q_col0 // gw + kv)),
                  pl.BlockSpec((s_len, LANES), lambda kv, i: (0, k_col0 // LANES + kv)),
                  pl.BlockSpec((s_len, LANES), lambda kv, i: (0, v_col0 // LANES + kv)),
                  pl.BlockSpec((tq, gw), lambda kv, i: (i, kv))],
        out_specs=pl.BlockSpec((tq, gw), lambda kv, i: (i, kv)),
        out_shape=jax.ShapeDtypeStruct((s_len, cfg.a_heads * LANES), BF16),
        compiler_params=_params("parallel", "arbitrary"),
        name="attn_global",
    )(q_arr, k_arr, v_arr, gate)


def _natten_geometry(rows):
    types, block_type = [], []
    for b in range(rows // NA_QB_ROWS):
        r = b * NA_QB_ROWS + np.arange(NA_QB_ROWS)
        u0 = int(np.clip(r[0] - NA_ROWS // 2, 0, rows - NA_UNION_ROWS))
        rs = np.clip(r - NA_ROWS // 2, 0, rows - NA_ROWS)
        assert u0 <= rs.min() and rs.max() + NA_ROWS <= u0 + NA_UNION_ROWS
        kr = u0 + np.arange(NA_UNION_ROWS)
        vr = (kr[None, :] >= rs[:, None]) & (kr[None, :] < rs[:, None] + NA_ROWS)
        dr = np.where(vr, kr[None, :] - r[:, None] + (NA_ROWS - 1), 0)
        for t, (dr_t, vr_t) in enumerate(types):
            if np.array_equal(dr, dr_t) and np.array_equal(vr, vr_t):
                block_type.append(t)
                break
        else:
            block_type.append(len(types))
            types.append((dr, vr))
    return types, block_type


def _build_natten_tables(rpb_ref, tbl_ref, types):
    c = lax.broadcasted_iota(jnp.int32, (GRID_W, LANES), 0)
    lane = lax.broadcasted_iota(jnp.int32, (GRID_W, LANES), 1)
    first = lane < GRID_W
    kc = jnp.where(first, lane, lane - GRID_W)
    cs = jnp.clip(c - NA_COLS // 2, 0, GRID_W - NA_COLS)
    vc = (kc >= cs) & (kc < cs + NA_COLS)
    base_shift = LANES - (NA_COLS - 1)
    for t, (dr, vr) in enumerate(types):
        for i in range(NA_QB_ROWS):
            for kp in range(NA_UNION_ROWS // 2):
                halves = []
                for half in range(2):
                    ku = 2 * kp + half
                    if vr[i, ku]:
                        row = jnp.broadcast_to(rpb_ref[int(dr[i, ku]):int(dr[i, ku]) + 1, :],
                                               (GRID_W, LANES))
                        halves.append(pltpu.roll(row, (base_shift + half * GRID_W) % LANES, 1,
                                                 stride=1, stride_axis=0))
                    else:
                        halves.append(None)
                lo, hi = halves
                if lo is None and hi is None:
                    tile = jnp.full((GRID_W, LANES), NEG_INF, F32)
                else:
                    ok = vc
                    if lo is None:
                        val, ok = hi, vc & ~first
                    elif hi is None:
                        val, ok = lo, vc & first
                    else:
                        val = jnp.where(first, lo, hi)
                    tile = jnp.where(ok, val * LOG2E, NEG_INF)
                tbl_ref[t, i * GRID_W:(i + 1) * GRID_W, kp * LANES:(kp + 1) * LANES] = tile


def _natten_kernel(q_ref, k_ref, v_ref, g_ref, rpb_ref, o_ref, tbl_ref, v1_ref, *, rows):
    qb = NA_QB_ROWS * GRID_W
    span = NA_UNION_ROWS * GRID_W
    nb_step = q_ref.shape[0] // qb
    types, block_type = _natten_geometry(rows)
    common = max(set(block_type), key=block_type.count)
    step = pl.program_id(1)

    @pl.when(step == 0)
    def _():
        _build_natten_tables(rpb_ref, tbl_ref, types)
        v1_ref[:, :LANES] = v_ref[...]
        v1_ref[:, LANES:] = jnp.ones((v_ref.shape[0], LANES), BF16)

    for b in range(nb_step):
        blk = step * nb_step + b
        u0 = jnp.clip(blk * NA_QB_ROWS - NA_ROWS // 2, 0, rows - NA_UNION_ROWS)
        typ = common
        for bb, t in enumerate(block_type):
            if t != common:
                typ = jnp.where(blk == bb, t, typ)
        koff = pl.multiple_of(u0 * GRID_W, GRID_W)
        qs = slice(b * qb, (b + 1) * qb)
        k = k_ref[pl.ds(koff, span), :]
        v1 = v1_ref[pl.ds(koff, span), :]
        s = lax.dot_general(q_ref[qs, :], k, (((1,), (1,)), ((), ())), preferred_element_type=F32)
        s = s + tbl_ref[typ]
        m = jnp.max(s, axis=-1, keepdims=True)
        pv = jnp.dot(jnp.exp2(s - m).astype(BF16), v1, preferred_element_type=F32)
        o_ref[qs, :] = (pv[:, :LANES] / pv[:, LANES:] * g_ref[qs, :]).astype(o_ref.dtype)


def _natten(q, k, v, gate, gate_col0, rpb, cfg):
    (q_arr, q_col0), (k_arr, k_col0), (v_arr, v_col0) = q, k, v
    s_len = q_arr.shape[0]
    rows = s_len // GRID_W
    tq = min(cfg.tq_b, s_len)
    qb = NA_QB_ROWS * GRID_W
    span = NA_UNION_ROWS * GRID_W
    assert rows % NA_QB_ROWS == 0 and rows >= NA_UNION_ROWS and tq % qb == 0
    n_types = len(_natten_geometry(rows)[0])
    nh, nr, nc = rpb.shape
    rpb_pad = jnp.pad(rpb.astype(F32), ((0, 0), (0, 2 * NA_ROWS - nr), (0, LANES - nc)))
    return pl.pallas_call(
        functools.partial(_natten_kernel, rows=rows),
        grid=(cfg.b_heads, s_len // tq),
        in_specs=[pl.BlockSpec((tq, LANES), lambda h, i: (i, q_col0 // LANES + h)),
                  pl.BlockSpec((s_len, LANES), lambda h, i: (0, k_col0 // LANES + h)),
                  pl.BlockSpec((s_len, LANES), lambda h, i: (0, v_col0 // LANES + h)),
                  pl.BlockSpec((tq, LANES), lambda h, i: (i, gate_col0 // LANES + h)),
                  pl.BlockSpec((None, 2 * NA_ROWS, LANES), lambda h, i: (h, 0, 0))],
        out_specs=pl.BlockSpec((tq, LANES), lambda h, i: (i, h)),
        out_shape=jax.ShapeDtypeStruct((s_len, cfg.b_heads * LANES), BF16),
        scratch_shapes=[pltpu.VMEM((n_types, qb, span), F32),
                        pltpu.VMEM((s_len, 2 * LANES), BF16)],
        compiler_params=_params("parallel", "arbitrary"),
        name="attn_neighbourhood",
    )(q_arr, k_arr, v_arr, gate, rpb_pad)


def _attn_c_kernel(slope_ref, sink_ref, q_ref, k_ref, v_ref, g_ref, o_ref, v1_ref, bias_ref, *, group, sub):
    tq = q_ref.shape[0]
    s_len = k_ref.shape[0]
    span = sub + 2 * C_WINDOW
    kv = pl.program_id(0)

    n_place = 3

    @pl.when(pl.program_id(1) == 0)
    def _():
        v1_ref[:, :LANES] = v_ref[...]
        v1_ref[:, LANES:] = jnp.ones((s_len, LANES), BF16)
        row = lax.broadcasted_iota(jnp.int32, (sub, span), 0)
        col = lax.broadcasted_iota(jnp.int32, (sub, span), 1)
        for place in range(n_place):
            dist = jnp.abs(row - col + place * C_WINDOW)
            pen = jnp.where(dist <= C_WINDOW, dist.astype(F32), -NEG_INF)
            for g in range(group):
                bias_ref[place * group + g] = -(slope_ref[kv * group + g] * LOG2E) * pen

    sinks = [sink_ref[kv * group + g] * LOG2E for g in range(group)]
    for b in range(tq // sub):
        t0 = pl.program_id(1) * tq + b * sub
        start = pl.multiple_of(jnp.clip(t0 - C_WINDOW, 0, s_len - span), C_WINDOW)
        place = (t0 - start) // C_WINDOW
        k = k_ref[pl.ds(start, span), :]
        v1 = v1_ref[pl.ds(start, span), :]
        rows = slice(b * sub, (b + 1) * sub)
        q = jnp.concatenate([q_ref[rows, g * LANES:(g + 1) * LANES] for g in range(group)], axis=0)
        s = lax.dot_general(q, k, (((1,), (1,)), ((), ())), preferred_element_type=F32)
        ps, ms = [], []
        for g in range(group):
            sg = s[g * sub:(g + 1) * sub, :] + bias_ref[place * group + g]
            m = jnp.maximum(jnp.max(sg, axis=-1, keepdims=True), sinks[g])
            ps.append(jnp.exp2(sg - m).astype(BF16))
            ms.append(m)
        pv = jnp.dot(jnp.concatenate(ps, axis=0), v1, preferred_element_type=F32)
        for g in range(group):
            sl = slice(g * LANES, (g + 1) * LANES)
            pg = pv[g * sub:(g + 1) * sub, :]
            l = pg[:, LANES:] + jnp.exp2(sinks[g] - ms[g])
            o_ref[rows, sl] = (pg[:, :LANES] / l * g_ref[rows, sl]).astype(o_ref.dtype)


def _attn_c(q, k, v, gate, slopes, sink, cfg):
    (q_arr, q_col0), (k_arr, k_col0), (v_arr, v_col0) = q, k, v
    s_len = q_arr.shape[0]
    group = cfg.c_heads // cfg.c_kv_heads
    gw = group * LANES
    tq, sub = cfg.tq_c, cfg.sub_c
    assert s_len >= sub + 2 * C_WINDOW and s_len % tq == 0 and tq % sub == 0 and sub % C_WINDOW == 0
    smem = pl.BlockSpec(memory_space=pltpu.SMEM)
    return pl.pallas_call(
        functools.partial(_attn_c_kernel, group=group, sub=sub),
        grid=(cfg.c_kv_heads, s_len // tq),
        scratch_shapes=[pltpu.VMEM((s_len, 2 * LANES), BF16),
                        pltpu.VMEM((3 * group, sub, sub + 2 * C_WINDOW), F32)],
        in_specs=[smem, smem,
                  pl.BlockSpec((tq, gw), lambda kv, i: (i, q_col0 // gw + kv)),
                  pl.BlockSpec((s_len, LANES), lambda kv, i: (0, k_col0 // LANES + kv)),
                  pl.BlockSpec((s_len, LANES), lambda kv, i: (0, v_col0 // LANES + kv)),
                  pl.BlockSpec((tq, gw), lambda kv, i: (i, kv))],
        out_specs=pl.BlockSpec((tq, gw), lambda kv, i: (i, kv)),
        out_shape=jax.ShapeDtypeStruct((s_len, cfg.c_heads * LANES), BF16),
        compiler_params=_params("parallel", "arbitrary"),
        name="attn_window",
    )(slopes, sink, q_arr, k_arr, v_arr, gate)


def _rope_tables(s_len):
    half = LANES // 2
    quarter = half // 2
    inv = jnp.exp(-math.log(ROPE_THETA) * jnp.arange(0, half, 2, dtype=F32) / half)
    pos = jnp.arange(s_len, dtype=jnp.int32)
    rows = (pos // GRID_W).astype(F32)
    cols = (pos % GRID_W).astype(F32)
    ang = jnp.concatenate([rows[:, None] * inv[None, :]] * 2 + [cols[:, None] * inv[None, :]] * 2, axis=1)
    first = (np.arange(LANES) % half) < quarter
    cos, sin = jnp.cos(ang), jnp.sin(ang)
    return cos, jnp.where(first[None], -sin, 0.0), jnp.where(first[None], 0.0, sin)


def _forward(x, norm_w, w_in_ab, w_out_ab, q_norm_a, k_norm_a, rpb_b, w_in_c, w_out_c, sink_c,
             final_norm_w, cfg):
    bsz, s_len, d = x.shape
    assert bsz == 1 and s_len % GRID_W == 0
    depth = norm_w.shape[0]
    scale = LANES ** -0.5 * LOG2E
    a_q, a_kv, b_w = cfg.a_heads * LANES, cfg.a_kv_heads * LANES, cfg.b_heads * LANES
    c_q, c_kv = cfg.c_heads * LANES, cfg.c_kv_heads * LANES
    rope = _rope_tables(s_len)
    slopes = jnp.exp2(-8.0 * jnp.arange(1, cfg.c_heads + 1, dtype=F32) / cfg.c_heads)
    xs = x.reshape(s_len, d)
    h, ssq = _prep(xs, cfg.t_norm)
    for layer in range(depth):
        lnw = norm_w[layer]
        if layer % 2 == 0:
            e = layer // 2
            n_rope = a_q + a_kv
            zr = _rope_proj(h, ssq, lnw, w_in_ab, e, a_q, a_kv, scale, rope, q_norm_a[e], k_norm_a[e], cfg)
            zc = _cast_proj(h, ssq, lnw, w_in_ab, e, n_rope, ((a_kv, 1.0), (b_w, scale), (2 * b_w, 1.0)),
                            cfg)
            gate = _gate_proj(h, ssq, lnw, w_in_ab, e, n_rope + a_kv + 3 * b_w, a_q + b_w, cfg)
            ya = _attn_a((zr, 0), (zr, a_q), (zc, 0), gate, cfg)
            yb = _natten((zc, a_kv), (zc, a_kv + b_w), (zc, a_kv + 2 * b_w), gate, a_q, rpb_b[e], cfg)
            xs, h, ssq = _outproj([ya, yb], w_out_ab, e, xs, cfg)
        else:
            o = layer // 2
            zc = _cast_proj(h, ssq, lnw, w_in_c, o, 0, ((c_q, scale), (2 * c_kv, 1.0)), cfg)
            gate = _gate_proj(h, ssq, lnw, w_in_c, o, c_q + 2 * c_kv, c_q, cfg)
            y = _attn_c((zc, 0), (zc, c_q), (zc, c_q + c_kv), gate, slopes, sink_c[o].astype(F32), cfg)
            xs, h, ssq = _outproj([y], w_out_c, o, xs, cfg)
    out = _rmsnorm(xs, final_norm_w, x.dtype, cfg.t_norm)
    return out.reshape(bsz, s_len, d)


def kernel(x, norm_w, w_in_ab, w_out_ab, q_norm_a, k_norm_a, rpb_b, w_in_c, w_out_c, sink_c, final_norm_w):
    return _forward(x, norm_w, w_in_ab, w_out_ab, q_norm_a, k_norm_a, rpb_b, w_in_c, w_out_c, sink_c,
                    final_norm_w, Config())
```

```python
import functools
import math
from typing import NamedTuple

import jax
import jax.numpy as jnp
import numpy as np
from jax import lax
from jax.experimental import pallas as pl
from jax.experimental.pallas import tpu as pltpu

F32 = jnp.float32
BF16 = jnp.bfloat16

LANES = 128
NORM_EPS = 1e-6
NEG_INF = -1e30
LOG2E = math.log2(math.e)
ROPE_THETA = 10000.0
GRID_W = 64
NA_ROWS = 8
NA_COLS = 16
NA_QB_ROWS = 4
NA_UNION_ROWS = 12
C_WINDOW = 128
PROJ_SPLIT = 8
A_SAFE_SHIFT = 60.0
VMEM_LIMIT_BYTES = 56 * 1024 * 1024


class Config(NamedTuple):
    a_heads: int = 16
    a_kv_heads: int = 4
    b_heads: int = 16
    c_heads: int = 32
    c_kv_heads: int = 8
    tm: int = 1024
    tn: int = 512
    tq_a: int = 256
    tk_a: int = 1024
    chunk_a: int = 16
    tq_c: int = 512
    sub_c: int = 128
    tq_b: int = 2048
    t_norm: int = 256


def _params(*sem):
    return pltpu.CompilerParams(dimension_semantics=sem, vmem_limit_bytes=VMEM_LIMIT_BYTES)


def _rmsnorm_kernel(x_ref, w_ref, o_ref):
    x = x_ref[...]
    ms = jnp.mean(x * x, axis=-1, keepdims=True)
    o_ref[...] = (x * lax.rsqrt(ms + NORM_EPS) * w_ref[...]).astype(o_ref.dtype)


def _rmsnorm(x, w, out_dtype, t):
    m, d = x.shape
    return pl.pallas_call(
        _rmsnorm_kernel,
        grid=(m // t,),
        in_specs=[pl.BlockSpec((t, d), lambda i: (i, 0)),
                  pl.BlockSpec((1, d), lambda i: (0, 0))],
        out_specs=pl.BlockSpec((t, d), lambda i: (i, 0)),
        out_shape=jax.ShapeDtypeStruct((m, d), out_dtype),
        compiler_params=_params("parallel"),
        name="rmsnorm",
    )(x, w.reshape(1, d))


def _lane_partial_sumsq(x):
    sq = x * x
    acc = sq[:, :LANES]
    for c in range(1, x.shape[1] // LANES):
        acc = acc + sq[:, c * LANES:(c + 1) * LANES]
    return acc


def _prep_kernel(x_ref, xb_ref, ssq_ref):
    x = x_ref[...]
    xb_ref[...] = x.astype(xb_ref.dtype)
    ssq_ref[...] = _lane_partial_sumsq(x)


def _prep(x, t):
    m, d = x.shape
    return pl.pallas_call(
        _prep_kernel,
        grid=(m // t,),
        in_specs=[pl.BlockSpec((t, d), lambda i: (i, 0))],
        out_specs=[pl.BlockSpec((t, d), lambda i: (i, 0)), pl.BlockSpec((t, LANES), lambda i: (i, 0))],
        out_shape=[jax.ShapeDtypeStruct((m, d), BF16), jax.ShapeDtypeStruct((m, LANES), F32)],
        compiler_params=_params("parallel"),
        name="prep",
    )(x)


def _weight_copy(w_hbm, wbuf_ref, sem_ref, e, tile, slot):
    tn = wbuf_ref.shape[2]
    cols = pl.ds(pl.multiple_of(tile * tn, tn), tn)
    return pltpu.make_async_copy(w_hbm.at[e, :, cols], wbuf_ref.at[slot], sem_ref.at[slot])


def _stage_weight(w_hbm, wbuf_ref, sem_ref, wbf_ref, e, tile0, row_scale_ref=None):
    j, nj = pl.program_id(0), pl.num_programs(0)

    @pl.when(pl.program_id(1) == 0)
    def _():
        slot = j % 2

        @pl.when(j == 0)
        def _():
            _weight_copy(w_hbm, wbuf_ref, sem_ref, e, tile0, 0).start()

        @pl.when(j + 1 < nj)
        def _():
            _weight_copy(w_hbm, wbuf_ref, sem_ref, e, tile0 + j + 1, 1 - slot).start()

        _weight_copy(w_hbm, wbuf_ref, sem_ref, e, tile0 + j, slot).wait()
        w = wbuf_ref[slot]
        wbf_ref[...] = (w if row_scale_ref is None else w * row_scale_ref[...]).astype(BF16)


def _weight_scratch(k, tn):
    return [pltpu.VMEM((2, k, tn), F32), pltpu.SemaphoreType.DMA((2,)), pltpu.VMEM((k, tn), BF16)]


W_HBM_SPEC = pl.BlockSpec(memory_space=pl.ANY)


def _norm_rope(z, w, cos, sin_lo, sin_hi):
    ms = jnp.mean(z * z, axis=-1, keepdims=True)
    y = z * lax.rsqrt(ms + NORM_EPS) * w
    quarter = LANES // 4
    return (y * cos + pltpu.roll(y, LANES - quarter, 1) * sin_lo
            + pltpu.roll(y, quarter, 1) * sin_hi)


def _rinv(ssq_ref, d):
    return lax.rsqrt(jnp.sum(ssq_ref[...], axis=-1, keepdims=True) * (1.0 / d) + NORM_EPS)


def _row_blocks(tm):
    step = tm // PROJ_SPLIT
    return [slice(r * step, (r + 1) * step) for r in range(PROJ_SPLIT)]


def _rope_proj_kernel(h_ref, ssq_ref, lnw_ref, w_hbm, cos_ref, slo_ref, shi_ref, qn_ref, kn_ref, o_ref,
                      wbuf_ref, sem_ref, wbf_ref, *, e, q_tiles, q_scale):
    _stage_weight(w_hbm, wbuf_ref, sem_ref, wbf_ref, e, 0, lnw_ref)
    nw = jnp.where(pl.program_id(0) < q_tiles, qn_ref[...] * q_scale, kn_ref[...])
    rinv = _rinv(ssq_ref, h_ref.shape[1])
    for rs in _row_blocks(h_ref.shape[0]):
        acc = jnp.dot(h_ref[rs, :], wbf_ref[...], preferred_element_type=F32) * rinv[rs, :]
        cos, slo, shi = cos_ref[rs, :], slo_ref[rs, :], shi_ref[rs, :]
        for hh in range(acc.shape[1] // LANES):
            sl = slice(hh * LANES, (hh + 1) * LANES)
            o_ref[rs, sl] = _norm_rope(acc[:, sl], nw, cos, slo, shi).astype(o_ref.dtype)


def _rope_proj(h, ssq, lnw, w, e, n_q, n_k, q_scale, rope, qn, kn, cfg):
    m, k = h.shape
    tm, tn = min(cfg.tm, m), cfg.tn
    row = lambda j, i: (i, 0)
    vec = pl.BlockSpec((1, LANES), lambda j, i: (0, 0))
    return pl.pallas_call(
        functools.partial(_rope_proj_kernel, e=e, q_tiles=n_q // tn, q_scale=q_scale),
        grid=((n_q + n_k) // tn, m // tm),
        in_specs=[pl.BlockSpec((tm, k), row), pl.BlockSpec((tm, LANES), row),
                  pl.BlockSpec((k, 1), lambda j, i: (0, 0)), W_HBM_SPEC,
                  pl.BlockSpec((tm, LANES), row), pl.BlockSpec((tm, LANES), row),
                  pl.BlockSpec((tm, LANES), row), vec, vec],
        out_specs=pl.BlockSpec((tm, tn), lambda j, i: (i, j)),
        out_shape=jax.ShapeDtypeStruct((m, n_q + n_k), BF16),
        scratch_shapes=_weight_scratch(k, tn),
        compiler_params=_params("arbitrary", "arbitrary"),
        name="ropeproj",
    )(h, ssq, lnw.reshape(k, 1), w, *rope, qn.reshape(1, LANES), kn.reshape(1, LANES))


def _cast_proj_kernel(scale_ref, h_ref, ssq_ref, lnw_ref, w_hbm, o_ref, wbuf_ref, sem_ref, wbf_ref, *,
                      e, tile0):
    _stage_weight(w_hbm, wbuf_ref, sem_ref, wbf_ref, e, tile0, lnw_ref)
    scale = _rinv(ssq_ref, h_ref.shape[1]) * scale_ref[pl.program_id(0)]
    for rs in _row_blocks(h_ref.shape[0]):
        acc = jnp.dot(h_ref[rs, :], wbf_ref[...], preferred_element_type=F32)
        o_ref[rs, :] = (acc * scale[rs, :]).astype(o_ref.dtype)


def _cast_proj(h, ssq, lnw, w, e, col0, widths_scales, cfg):
    m, k = h.shape
    tm, tn = min(cfg.tm, m), cfg.tn
    scales = np.concatenate([np.full(width // tn, scale, np.float32) for width, scale in widths_scales])
    n_out = tn * len(scales)
    return pl.pallas_call(
        functools.partial(_cast_proj_kernel, e=e, tile0=col0 // tn),
        grid=(n_out // tn, m // tm),
        in_specs=[pl.BlockSpec(memory_space=pltpu.SMEM),
                  pl.BlockSpec((tm, k), lambda j, i: (i, 0)),
                  pl.BlockSpec((tm, LANES), lambda j, i: (i, 0)),
                  pl.BlockSpec((k, 1), lambda j, i: (0, 0)), W_HBM_SPEC],
        out_specs=pl.BlockSpec((tm, tn), lambda j, i: (i, j)),
        out_shape=jax.ShapeDtypeStruct((m, n_out), BF16),
        scratch_shapes=_weight_scratch(k, tn),
        compiler_params=_params("arbitrary", "arbitrary"),
        name="castproj",
    )(jnp.asarray(scales), h, ssq, lnw.reshape(k, 1), w)


def _gate_kernel(h_ref, ssq_ref, lnw_ref, w_hbm, o_ref, wbuf_ref, sem_ref, wbf_ref, *, e, tile0):
    _stage_weight(w_hbm, wbuf_ref, sem_ref, wbf_ref, e, tile0, lnw_ref)
    rinv = _rinv(ssq_ref, h_ref.shape[1])
    for rs in _row_blocks(h_ref.shape[0]):
        z = jnp.dot(h_ref[rs, :], wbf_ref[...], preferred_element_type=F32) * rinv[rs, :]
        o_ref[rs, :] = (z * (0.5 * jnp.tanh(0.5 * z) + 0.5)).astype(o_ref.dtype)


def _gate_proj(h, ssq, lnw, w, e, col0, n_out, cfg):
    m, k = h.shape
    tm, tn = min(cfg.tm, m), cfg.tn
    return pl.pallas_call(
        functools.partial(_gate_kernel, e=e, tile0=col0 // tn),
        grid=(n_out // tn, m // tm),
        in_specs=[pl.BlockSpec((tm, k), lambda j, i: (i, 0)),
                  pl.BlockSpec((tm, LANES), lambda j, i: (i, 0)),
                  pl.BlockSpec((k, 1), lambda j, i: (0, 0)), W_HBM_SPEC],
        out_specs=pl.BlockSpec((tm, tn), lambda j, i: (i, j)),
        out_shape=jax.ShapeDtypeStruct((m, n_out), BF16),
        scratch_shapes=_weight_scratch(k, tn),
        compiler_params=_params("arbitrary", "arbitrary"),
        name="gateproj",
    )(h, ssq, lnw.reshape(k, 1), w)


def _outproj_kernel(*refs, n_lhs, e):
    y_refs = refs[:n_lhs]
    w_hbm, x_ref, o_ref, ob_ref, ssq_hbm, wbuf_ref, sem_ref, wbf_ref, ssq_acc_ref, ssq_sem = refs[n_lhs:]
    _stage_weight(w_hbm, wbuf_ref, sem_ref, wbf_ref, e, 0)
    acc = x_ref[...]
    k0 = 0
    for y_ref in y_refs:
        kk = y_ref.shape[1]
        acc = acc + jnp.dot(y_ref[...], wbf_ref[k0:k0 + kk, :], preferred_element_type=F32)
        k0 += kk
    o_ref[...] = acc
    ob_ref[...] = acc.astype(ob_ref.dtype)
    tm = acc.shape[0]
    rows = pl.ds(pl.multiple_of(pl.program_id(1) * tm, tm), tm)
    part = _lane_partial_sumsq(acc)

    @pl.when(pl.program_id(0) == 0)
    def _():
        ssq_acc_ref[rows, :] = part

    @pl.when(pl.program_id(0) > 0)
    def _():
        ssq_acc_ref[rows, :] += part

    @pl.when(pl.program_id(0) == pl.num_programs(0) - 1)
    def _():
        copy = pltpu.make_async_copy(ssq_acc_ref.at[rows, :], ssq_hbm.at[rows, :], ssq_sem.at[0])
        copy.start()
        copy.wait()


def _outproj(ys, w, e, x, cfg):
    m, n = x.shape
    k = w.shape[1]
    tm, tn = min(cfg.tm, m), cfg.tn
    in_specs = [pl.BlockSpec((tm, y.shape[1]), lambda j, i: (i, 0)) for y in ys]
    in_specs += [W_HBM_SPEC, pl.BlockSpec((tm, tn), lambda j, i: (i, j))]
    return pl.pallas_call(
        functools.partial(_outproj_kernel, n_lhs=len(ys), e=e),
        grid=(n // tn, m // tm),
        in_specs=in_specs,
        out_specs=[pl.BlockSpec((tm, tn), lambda j, i: (i, j)),
                   pl.BlockSpec((tm, tn), lambda j, i: (i, j)),
                   pl.BlockSpec(memory_space=pl.ANY)],
        out_shape=[jax.ShapeDtypeStruct((m, n), F32), jax.ShapeDtypeStruct((m, n), BF16),
                   jax.ShapeDtypeStruct((m, LANES), F32)],
        scratch_shapes=_weight_scratch(k, tn) + [pltpu.VMEM((m, LANES), F32),
                                                 pltpu.SemaphoreType.DMA((1,))],
        compiler_params=_params("arbitrary", "arbitrary"),
        name="outproj",
    )(*ys, w, x)


def _attn_a_kernel(q_ref, k_ref, v_ref, g_ref, o_ref, qs_ref, v1_ref, s_ref, p_ref, m_ref, alpha_ref,
                   acc_ref, kmax_ref, *, tk, group, chunk):
    tq = q_ref.shape[0]
    s_len = k_ref.shape[0]
    nk = s_len // tk

    @pl.when(pl.program_id(1) == 0)
    def _():
        v1_ref[:, :LANES] = v_ref[...]
        v1_ref[:, LANES:] = jnp.ones((s_len, LANES), BF16)
        kf = k_ref[...].astype(F32)
        kmax_ref[0] = jnp.max(jnp.sum(kf * kf, axis=-1, keepdims=True))

    for g in range(group):
        qs_ref[g * tq:(g + 1) * tq, :] = q_ref[:, g * LANES:(g + 1) * LANES]
    rows = group * tq

    def scores(t):
        k = k_ref[t * tk:(t + 1) * tk, :]
        return lax.dot_general(qs_ref[...], k, (((1,), (1,)), ((), ())), preferred_element_type=F32)

    qf = qs_ref[...].astype(F32)
    bound = jnp.sqrt(jnp.sum(qf * qf, axis=-1, keepdims=True) * kmax_ref[0])

    def fixed_shift():
        acc_ref[...] = jnp.zeros((rows, 2 * LANES), F32)
        for t in range(nk):
            p = jnp.exp2(scores(t) - bound).astype(BF16)
            acc_ref[...] += jnp.dot(p, v1_ref[t * tk:(t + 1) * tk, :], preferred_element_type=F32)

    def running_max():
        m_ref[...] = jnp.full((rows, 1), NEG_INF, F32)
        acc_ref[...] = jnp.zeros((rows, 2 * LANES), F32)
        s_ref[0] = scores(0)
        for t in range(nk):
            if t + 1 < nk:
                s_ref[(t + 1) % 2] = scores(t + 1)
            for c in range(rows // chunk):
                rs = slice(c * chunk, (c + 1) * chunk)
                s = s_ref[t % 2, rs, :]
                m_old = m_ref[rs, :]
                m_new = jnp.maximum(m_old, jnp.max(s, axis=-1, keepdims=True))
                m_ref[rs, :] = m_new
                alpha_ref[rs, :] = jnp.exp2(m_old - m_new)
                p_ref[t % 2, rs, :] = jnp.exp2(s - m_new).astype(BF16)
            pv = jnp.dot(p_ref[t % 2], v1_ref[t * tk:(t + 1) * tk, :], preferred_element_type=F32)
            acc_ref[...] = alpha_ref[...] * acc_ref[...] + pv

    lax.cond(jnp.max(bound) <= A_SAFE_SHIFT, fixed_shift, running_max)
    o = acc_ref[:, :LANES] / acc_ref[:, LANES:]
    for g in range(group):
        sl = slice(g * LANES, (g + 1) * LANES)
        o_ref[:, sl] = (o[g * tq:(g + 1) * tq, :] * g_ref[:, sl]).astype(o_ref.dtype)


def _attn_a(q, k, v, gate, cfg):
    (q_arr, q_col0), (k_arr, k_col0), (v_arr, v_col0) = q, k, v
    s_len = q_arr.shape[0]
    group = cfg.a_heads // cfg.a_kv_heads
    gw = group * LANES
    tq = min(cfg.tq_a, s_len)
    tk = min(cfg.tk_a, s_len)
    return pl.pallas_call(
        functools.partial(_attn_a_kernel, tk=tk, group=group, chunk=cfg.chunk_a),
        grid=(cfg.a_kv_heads, s_len // tq),
        scratch_shapes=[pltpu.VMEM((group * tq, LANES), BF16),
                        pltpu.VMEM((s_len, 2 * LANES), BF16),
                        pltpu.VMEM((2, group * tq, tk), F32),
                        pltpu.VMEM((2, group * tq, tk), BF16),
                        pltpu.VMEM((group * tq, 1), F32),
                        pltpu.VMEM((group * tq, 1), F32),
                        pltpu.VMEM((group * tq, 2 * LANES), F32),
                        pltpu.SMEM((1,), F32)],
        in_specs=[pl.BlockSpec((tq, gw), lambda kv, i: (i, q_col0 // gw + kv)),
                  pl.BlockSpec((s_len, LANES), lambda kv, i: (0, k_col0 // LANES + kv)),
                  pl.BlockSpec((s_len, LANES), lambda kv, i: (0, v_col0 // LANES + kv)),
                  pl.BlockSpec((tq, gw), lambda kv, i: (i, kv))],
        out_specs=pl.BlockSpec((tq, gw), lambda kv, i: (i, kv)),
        out_shape=jax.ShapeDtypeStruct((s_len, cfg.a_heads * LANES), BF16),
        compiler_params=_params("parallel", "arbitrary"),
        name="attn_global",
    )(q_arr, k_arr, v_arr, gate)


def _natten_geometry(rows):
    types, block_type = [], []
    for b in range(rows // NA_QB_ROWS):
        r = b * NA_QB_ROWS + np.arange(NA_QB_ROWS)
        u0 = int(np.clip(r[0] - NA_ROWS // 2, 0, rows - NA_UNION_ROWS))
        rs = np.clip(r - NA_ROWS // 2, 0, rows - NA_ROWS)
        assert u0 <= rs.min() and rs.max() + NA_ROWS <= u0 + NA_UNION_ROWS
        kr = u0 + np.arange(NA_UNION_ROWS)
        vr = (kr[None, :] >= rs[:, None]) & (kr[None, :] < rs[:, None] + NA_ROWS)
        dr = np.where(vr, kr[None, :] - r[:, None] + (NA_ROWS - 1), 0)
        for t, (dr_t, vr_t) in enumerate(types):
            if np.array_equal(dr, dr_t) and np.array_equal(vr, vr_t):
                block_type.append(t)
                break
        else:
            block_type.append(len(types))
            types.append((dr, vr))
    return types, block_type


def _build_natten_tables(rpb_ref, tbl_ref, types):
    c = lax.broadcasted_iota(jnp.int32, (GRID_W, LANES), 0)
    lane = lax.broadcasted_iota(jnp.int32, (GRID_W, LANES), 1)
    first = lane < GRID_W
    kc = jnp.where(first, lane, lane - GRID_W)
    cs = jnp.clip(c - NA_COLS // 2, 0, GRID_W - NA_COLS)
    vc = (kc >= cs) & (kc < cs + NA_COLS)
    base_shift = LANES - (NA_COLS - 1)
    for t, (dr, vr) in enumerate(types):
        for i in range(NA_QB_ROWS):
            for kp in range(NA_UNION_ROWS // 2):
                halves = []
                for half in range(2):
                    ku = 2 * kp + half
                    if vr[i, ku]:
                        row = jnp.broadcast_to(rpb_ref[int(dr[i, ku]):int(dr[i, ku]) + 1, :],
                                               (GRID_W, LANES))
                        halves.append(pltpu.roll(row, (base_shift + half * GRID_W) % LANES, 1,
                                                 stride=1, stride_axis=0))
                    else:
                        halves.append(None)
                lo, hi = halves
                if lo is None and hi is None:
                    tile = jnp.full((GRID_W, LANES), NEG_INF, F32)
                else:
                    ok = vc
                    if lo is None:
                        val, ok = hi, vc & ~first
                    elif hi is None:
                        val, ok = lo, vc & first
                    else:
                        val = jnp.where(first, lo, hi)
                    tile = jnp.where(ok, val * LOG2E, NEG_INF)
                tbl_ref[t, i * GRID_W:(i + 1) * GRID_W, kp * LANES:(kp + 1) * LANES] = tile


def _natten_kernel(q_ref, k_ref, v_ref, g_ref, rpb_ref, o_ref, tbl_ref, v1_ref, *, rows):
    qb = NA_QB_ROWS * GRID_W
    span = NA_UNION_ROWS * GRID_W
    nb_step = q_ref.shape[0] // qb
    types, block_type = _natten_geometry(rows)
    common = max(set(block_type), key=block_type.count)
    step = pl.program_id(1)

    @pl.when(step == 0)
    def _():
        _build_natten_tables(rpb_ref, tbl_ref, types)
        v1_ref[:, :LANES] = v_ref[...]
        v1_ref[:, LANES:] = jnp.ones((v_ref.shape[0], LANES), BF16)

    for b in range(nb_step):
        blk = step * nb_step + b
        u0 = jnp.clip(blk * NA_QB_ROWS - NA_ROWS // 2, 0, rows - NA_UNION_ROWS)
        typ = common
        for bb, t in enumerate(block_type):
            if t != common:
                typ = jnp.where(blk == bb, t, typ)
        koff = pl.multiple_of(u0 * GRID_W, GRID_W)
        qs = slice(b * qb, (b + 1) * qb)
        k = k_ref[pl.ds(koff, span), :]
        v1 = v1_ref[pl.ds(koff, span), :]
        s = lax.dot_general(q_ref[qs, :], k, (((1,), (1,)), ((), ())), preferred_element_type=F32)
        s = s + tbl_ref[typ]
        m = jnp.max(s, axis=-1, keepdims=True)
        pv = jnp.dot(jnp.exp2(s - m).astype(BF16), v1, preferred_element_type=F32)
        o_ref[qs, :] = (pv[:, :LANES] / pv[:, LANES:] * g_ref[qs, :]).astype(o_ref.dtype)


def _natten(q, k, v, gate, gate_col0, rpb, cfg):
    (q_arr, q_col0), (k_arr, k_col0), (v_arr, v_col0) = q, k, v
    s_len = q_arr.shape[0]
    rows = s_len // GRID_W
    tq = min(cfg.tq_b, s_len)
    qb = NA_QB_ROWS * GRID_W
    span = NA_UNION_ROWS * GRID_W
    assert rows % NA_QB_ROWS == 0 and rows >= NA_UNION_ROWS and tq % qb == 0
    n_types = len(_natten_geometry(rows)[0])
    nh, nr, nc = rpb.shape
    rpb_pad = jnp.pad(rpb.astype(F32), ((0, 0), (0, 2 * NA_ROWS - nr), (0, LANES - nc)))
    return pl.pallas_call(
        functools.partial(_natten_kernel, rows=rows),
        grid=(cfg.b_heads, s_len // tq),
        in_specs=[pl.BlockSpec((tq, LANES), lambda h, i: (i, q_col0 // LANES + h)),
                  pl.BlockSpec((s_len, LANES), lambda h, i: (0, k_col0 // LANES + h)),
                  pl.BlockSpec((s_len, LANES), lambda h, i: (0, v_col0 // LANES + h)),
                  pl.BlockSpec((tq, LANES), lambda h, i: (i, gate_col0 // LANES + h)),
                  pl.BlockSpec((None, 2 * NA_ROWS, LANES), lambda h, i: (h, 0, 0))],
        out_specs=pl.BlockSpec((tq, LANES), lambda h, i: (i, h)),
        out_shape=jax.ShapeDtypeStruct((s_len, cfg.b_heads * LANES), BF16),
        scratch_shapes=[pltpu.VMEM((n_types, qb, span), F32),
                        pltpu.VMEM((s_len, 2 * LANES), BF16)],
        compiler_params=_params("parallel", "arbitrary"),
        name="attn_neighbourhood",
    )(q_arr, k_arr, v_arr, gate, rpb_pad)


def _attn_c_kernel(slope_ref, sink_ref, q_ref, k_ref, v_ref, g_ref, o_ref, v1_ref, bias_ref, *, group, sub):
    tq = q_ref.shape[0]
    s_len = k_ref.shape[0]
    span = sub + 2 * C_WINDOW
    kv = pl.program_id(0)

    n_place = 3

    @pl.when(pl.program_id(1) == 0)
    def _():
        v1_ref[:, :LANES] = v_ref[...]
        v1_ref[:, LANES:] = jnp.ones((s_len, LANES), BF16)
        row = lax.broadcasted_iota(jnp.int32, (sub, span), 0)
        col = lax.broadcasted_iota(jnp.int32, (sub, span), 1)
        for place in range(n_place):
            dist = jnp.abs(row - col + place * C_WINDOW)
            pen = jnp.where(dist <= C_WINDOW, dist.astype(F32), -NEG_INF)
            for g in range(group):
                bias_ref[place * group + g] = -(slope_ref[kv * group + g] * LOG2E) * pen

    sinks = [sink_ref[kv * group + g] * LOG2E for g in range(group)]
    for b in range(tq // sub):
        t0 = pl.program_id(1) * tq + b * sub
        start = pl.multiple_of(jnp.clip(t0 - C_WINDOW, 0, s_len - span), C_WINDOW)
        place = (t0 - start) // C_WINDOW
        k = k_ref[pl.ds(start, span), :]
        v1 = v1_ref[pl.ds(start, span), :]
        rows = slice(b * sub, (b + 1) * sub)
        q = jnp.concatenate([q_ref[rows, g * LANES:(g + 1) * LANES] for g in range(group)], axis=0)
        s = lax.dot_general(q, k, (((1,), (1,)), ((), ())), preferred_element_type=F32)
        ps, ms = [], []
        for g in range(group):
            sg = s[g * sub:(g + 1) * sub, :] + bias_ref[place * group + g]
            m = jnp.maximum(jnp.max(sg, axis=-1, keepdims=True), sinks[g])
            ps.append(jnp.exp2(sg - m).astype(BF16))
            ms.append(m)
        pv = jnp.dot(jnp.concatenate(ps, axis=0), v1, preferred_element_type=F32)
        for g in range(group):
            sl = slice(g * LANES, (g + 1) * LANES)
            pg = pv[g * sub:(g + 1) * sub, :]
            l = pg[:, LANES:] + jnp.exp2(sinks[g] - ms[g])
            o_ref[rows, sl] = (pg[:, :LANES] / l * g_ref[rows, sl]).astype(o_ref.dtype)


def _attn_c(q, k, v, gate, slopes, sink, cfg):
    (q_arr, q_col0), (k_arr, k_col0), (v_arr, v_col0) = q, k, v
    s_len = q_arr.shape[0]
    group = cfg.c_heads // cfg.c_kv_heads
    gw = group * LANES
    tq, sub = cfg.tq_c, cfg.sub_c
    assert s_len >= sub + 2 * C_WINDOW and s_len % tq == 0 and tq % sub == 0 and sub % C_WINDOW == 0
    smem = pl.BlockSpec(memory_space=pltpu.SMEM)
    return pl.pallas_call(
        functools.partial(_attn_c_kernel, group=group, sub=sub),
        grid=(cfg.c_kv_heads, s_len // tq),
        scratch_shapes=[pltpu.VMEM((s_len, 2 * LANES), BF16),
                        pltpu.VMEM((3 * group, sub, sub + 2 * C_WINDOW), F32)],
        in_specs=[smem, smem,
                  pl.BlockSpec((tq, gw), lambda kv, i: (i, q_col0 // gw + kv)),
                  pl.BlockSpec((s_len, LANES), lambda kv, i: (0, k_col0 // LANES + kv)),
                  pl.BlockSpec((s_len, LANES), lambda kv, i: (0, v_col0 // LANES + kv)),
                  pl.BlockSpec((tq, gw), lambda kv, i: (i, kv))],
        out_specs=pl.BlockSpec((tq, gw), lambda kv, i: (i, kv)),
        out_shape=jax.ShapeDtypeStruct((s_len, cfg.c_heads * LANES), BF16),
        compiler_params=_params("parallel", "arbitrary"),
        name="attn_window",
    )(slopes, sink, q_arr, k_arr, v_arr, gate)


def _rope_tables(s_len):
    half = LANES // 2
    quarter = half // 2
    inv = jnp.exp(-math.log(ROPE_THETA) * jnp.arange(0, half, 2, dtype=F32) / half)
    pos = jnp.arange(s_len, dtype=jnp.int32)
    rows = (pos // GRID_W).astype(F32)
    cols = (pos % GRID_W).astype(F32)
    ang = jnp.concatenate([rows[:, None] * inv[None, :]] * 2 + [cols[:, None] * inv[None, :]] * 2, axis=1)
    first = (np.arange(LANES) % half) < quarter
    cos, sin = jnp.cos(ang), jnp.sin(ang)
    return cos, jnp.where(first[None], -sin, 0.0), jnp.where(first[None], 0.0, sin)


def _forward(x, norm_w, w_in_ab, w_out_ab, q_norm_a, k_norm_a, rpb_b, w_in_c, w_out_c, sink_c,
             final_norm_w, cfg):
    bsz, s_len, d = x.shape
    assert bsz == 1 and s_len % GRID_W == 0
    depth = norm_w.shape[0]
    scale = LANES ** -0.5 * LOG2E
    a_q, a_kv, b_w = cfg.a_heads * LANES, cfg.a_kv_heads * LANES, cfg.b_heads * LANES
    c_q, c_kv = cfg.c_heads * LANES, cfg.c_kv_heads * LANES
    rope = _rope_tables(s_len)
    slopes = jnp.exp2(-8.0 * jnp.arange(1, cfg.c_heads + 1, dtype=F32) / cfg.c_heads)
    xs = x.reshape(s_len, d)
    h, ssq = _prep(xs, cfg.t_norm)
    for layer in range(depth):
        lnw = norm_w[layer]
        if layer % 2 == 0:
            e = layer // 2
            n_rope = a_q + a_kv
            zr = _rope_proj(h, ssq, lnw, w_in_ab, e, a_q, a_kv, scale, rope, q_norm_a[e], k_norm_a[e], cfg)
            zc = _cast_proj(h, ssq, lnw, w_in_ab, e, n_rope, ((a_kv, 1.0), (b_w, scale), (2 * b_w, 1.0)),
                            cfg)
            gate = _gate_proj(h, ssq, lnw, w_in_ab, e, n_rope + a_kv + 3 * b_w, a_q + b_w, cfg)
            ya = _attn_a((zr, 0), (zr, a_q), (zc, 0), gate, cfg)
            yb = _natten((zc, a_kv), (zc, a_kv + b_w), (zc, a_kv + 2 * b_w), gate, a_q, rpb_b[e], cfg)
            xs, h, ssq = _outproj([ya, yb], w_out_ab, e, xs, cfg)
        else:
            o = layer // 2
            zc = _cast_proj(h, ssq, lnw, w_in_c, o, 0, ((c_q, scale), (2 * c_kv, 1.0)), cfg)
            gate = _gate_proj(h, ssq, lnw, w_in_c, o, c_q + 2 * c_kv, c_q, cfg)
            y = _attn_c((zc, 0), (zc, c_q), (zc, c_q + c_kv), gate, slopes, sink_c[o].astype(F32), cfg)
            xs, h, ssq = _outproj([y], w_out_c, o, xs, cfg)
    out = _rmsnorm(xs, final_norm_w, x.dtype, cfg.t_norm)
    return out.reshape(bsz, s_len, d)


def kernel(x, norm_w, w_in_ab, w_out_ab, q_norm_a, k_norm_a, rpb_b, w_in_c, w_out_c, sink_c, final_norm_w):
    return _forward(x, norm_w, w_in_ab, w_out_ab, q_norm_a, k_norm_a, rpb_b, w_in_c, w_out_c, sink_c,
                    final_norm_w, Config())
```

```python
import functools
import math
from typing import NamedTuple

import jax
import jax.numpy as jnp
import numpy as np
from jax import lax
from jax.experimental import pallas as pl
from jax.experimental.pallas import tpu as pltpu

F32 = jnp.float32
BF16 = jnp.bfloat16

LANES = 128
NORM_EPS = 1e-6
NEG_INF = -1e30
LOG2E = math.log2(math.e)
ROPE_THETA = 10000.0
GRID_W = 64
NA_ROWS = 8
NA_COLS = 16
NA_QB_ROWS = 4
NA_UNION_ROWS = 12
C_WINDOW = 128
PROJ_SPLIT = 8
CAST_SPLIT = 4
A_SAFE_SHIFT = 60.0
VMEM_LIMIT_BYTES = 56 * 1024 * 1024


class Config(NamedTuple):
    a_heads: int = 16
    a_kv_heads: int = 4
    b_heads: int = 16
    c_heads: int = 32
    c_kv_heads: int = 8
    tm: int = 1024
    tn: int = 512
    tq_a: int = 256
    tk_a: int = 1024
    chunk_a: int = 16
    tq_c: int = 1024
    sub_c: int = 128
    tq_b: int = 2048
    t_norm: int = 256


def _params(*sem):
    return pltpu.CompilerParams(dimension_semantics=sem, vmem_limit_bytes=VMEM_LIMIT_BYTES)


def _rmsnorm_kernel(x_ref, w_ref, o_ref):
    x = x_ref[...]
    ms = jnp.mean(x * x, axis=-1, keepdims=True)
    o_ref[...] = (x * lax.rsqrt(ms + NORM_EPS) * w_ref[...]).astype(o_ref.dtype)


def _rmsnorm(x, w, out_dtype, t):
    m, d = x.shape
    return pl.pallas_call(
        _rmsnorm_kernel,
        grid=(m // t,),
        in_specs=[pl.BlockSpec((t, d), lambda i: (i, 0)),
                  pl.BlockSpec((1, d), lambda i: (0, 0))],
        out_specs=pl.BlockSpec((t, d), lambda i: (i, 0)),
        out_shape=jax.ShapeDtypeStruct((m, d), out_dtype),
        compiler_params=_params("parallel"),
        name="rmsnorm",
    )(x, w.reshape(1, d))


def _lane_partial_sumsq(x):
    sq = x * x
    acc = sq[:, :LANES]
    for c in range(1, x.shape[1] // LANES):
        acc = acc + sq[:, c * LANES:(c + 1) * LANES]
    return acc


def _prep_kernel(x_ref, xb_ref, ssq_ref):
    x = x_ref[...]
    xb_ref[...] = x.astype(xb_ref.dtype)
    ssq_ref[...] = _lane_partial_sumsq(x)


def _prep(x, t):
    m, d = x.shape
    return pl.pallas_call(
        _prep_kernel,
        grid=(m // t,),
        in_specs=[pl.BlockSpec((t, d), lambda i: (i, 0))],
        out_specs=[pl.BlockSpec((t, d), lambda i: (i, 0)), pl.BlockSpec((t, LANES), lambda i: (i, 0))],
        out_shape=[jax.ShapeDtypeStruct((m, d), BF16), jax.ShapeDtypeStruct((m, LANES), F32)],
        compiler_params=_params("parallel"),
        name="prep",
    )(x)


def _weight_copy(w_hbm, wbuf_ref, sem_ref, e, tile, slot):
    tn = wbuf_ref.shape[2]
    cols = pl.ds(pl.multiple_of(tile * tn, tn), tn)
    return pltpu.make_async_copy(w_hbm.at[e, :, cols], wbuf_ref.at[slot], sem_ref.at[slot])


def _stage_weight(w_hbm, wbuf_ref, sem_ref, wbf_ref, e, tile0, row_scale_ref=None):
    j, nj = pl.program_id(0), pl.num_programs(0)

    @pl.when(pl.program_id(1) == 0)
    def _():
        slot = j % 2

        @pl.when(j == 0)
        def _():
            _weight_copy(w_hbm, wbuf_ref, sem_ref, e, tile0, 0).start()

        @pl.when(j + 1 < nj)
        def _():
            _weight_copy(w_hbm, wbuf_ref, sem_ref, e, tile0 + j + 1, 1 - slot).start()

        _weight_copy(w_hbm, wbuf_ref, sem_ref, e, tile0 + j, slot).wait()
        w = wbuf_ref[slot]
        wbf_ref[...] = (w if row_scale_ref is None else w * row_scale_ref[...]).astype(BF16)


def _weight_scratch(k, tn):
    return [pltpu.VMEM((2, k, tn), F32), pltpu.SemaphoreType.DMA((2,)), pltpu.VMEM((k, tn), BF16)]


W_HBM_SPEC = pl.BlockSpec(memory_space=pl.ANY)


def _norm_rope(z, w, cos, sin_lo, sin_hi):
    ms = jnp.mean(z * z, axis=-1, keepdims=True)
    y = z * lax.rsqrt(ms + NORM_EPS) * w
    quarter = LANES // 4
    return (y * cos + pltpu.roll(y, LANES - quarter, 1) * sin_lo
            + pltpu.roll(y, quarter, 1) * sin_hi)


def _rinv(ssq_ref, d):
    return lax.rsqrt(jnp.sum(ssq_ref[...], axis=-1, keepdims=True) * (1.0 / d) + NORM_EPS)


def _row_blocks(tm, split=PROJ_SPLIT):
    step = tm // split
    return [slice(r * step, (r + 1) * step) for r in range(split)]


def _rope_proj_kernel(h_ref, ssq_ref, lnw_ref, w_hbm, cos_ref, slo_ref, shi_ref, qn_ref, kn_ref, o_ref,
                      wbuf_ref, sem_ref, wbf_ref, *, e, q_tiles, q_scale):
    _stage_weight(w_hbm, wbuf_ref, sem_ref, wbf_ref, e, 0, lnw_ref)
    nw = jnp.where(pl.program_id(0) < q_tiles, qn_ref[...] * q_scale, kn_ref[...])
    rinv = _rinv(ssq_ref, h_ref.shape[1])
    for rs in _row_blocks(h_ref.shape[0]):
        acc = jnp.dot(h_ref[rs, :], wbf_ref[...], preferred_element_type=F32) * rinv[rs, :]
        cos, slo, shi = cos_ref[rs, :], slo_ref[rs, :], shi_ref[rs, :]
        for hh in range(acc.shape[1] // LANES):
            sl = slice(hh * LANES, (hh + 1) * LANES)
            o_ref[rs, sl] = _norm_rope(acc[:, sl], nw, cos, slo, shi).astype(o_ref.dtype)


def _rope_proj(h, ssq, lnw, w, e, n_q, n_k, q_scale, rope, qn, kn, cfg):
    m, k = h.shape
    tm, tn = min(cfg.tm, m), cfg.tn
    row = lambda j, i: (i, 0)
    vec = pl.BlockSpec((1, LANES), lambda j, i: (0, 0))
    return pl.pallas_call(
        functools.partial(_rope_proj_kernel, e=e, q_tiles=n_q // tn, q_scale=q_scale),
        grid=((n_q + n_k) // tn, m // tm),
        in_specs=[pl.BlockSpec((tm, k), row), pl.BlockSpec((tm, LANES), row),
                  pl.BlockSpec((k, 1), lambda j, i: (0, 0)), W_HBM_SPEC,
                  pl.BlockSpec((tm, LANES), row), pl.BlockSpec((tm, LANES), row),
                  pl.BlockSpec((tm, LANES), row), vec, vec],
        out_specs=pl.BlockSpec((tm, tn), lambda j, i: (i, j)),
        out_shape=jax.ShapeDtypeStruct((m, n_q + n_k), BF16),
        scratch_shapes=_weight_scratch(k, tn),
        compiler_params=_params("arbitrary", "arbitrary"),
        name="ropeproj",
    )(h, ssq, lnw.reshape(k, 1), w, *rope, qn.reshape(1, LANES), kn.reshape(1, LANES))


def _cast_proj_kernel(scale_ref, h_ref, ssq_ref, lnw_ref, w_hbm, o_ref, wbuf_ref, sem_ref, wbf_ref, *,
                      e, tile0):
    _stage_weight(w_hbm, wbuf_ref, sem_ref, wbf_ref, e, tile0, lnw_ref)
    scale = _rinv(ssq_ref, h_ref.shape[1]) * scale_ref[pl.program_id(0)]
    for rs in _row_blocks(h_ref.shape[0], CAST_SPLIT):
        acc = jnp.dot(h_ref[rs, :], wbf_ref[...], preferred_element_type=F32)
        o_ref[rs, :] = (acc * scale[rs, :]).astype(o_ref.dtype)


def _cast_proj(h, ssq, lnw, w, e, col0, widths_scales, cfg):
    m, k = h.shape
    tm, tn = min(cfg.tm, m), cfg.tn
    scales = np.concatenate([np.full(width // tn, scale, np.float32) for width, scale in widths_scales])
    n_out = tn * len(scales)
    return pl.pallas_call(
        functools.partial(_cast_proj_kernel, e=e, tile0=col0 // tn),
        grid=(n_out // tn, m // tm),
        in_specs=[pl.BlockSpec(memory_space=pltpu.SMEM),
                  pl.BlockSpec((tm, k), lambda j, i: (i, 0)),
                  pl.BlockSpec((tm, LANES), lambda j, i: (i, 0)),
                  pl.BlockSpec((k, 1), lambda j, i: (0, 0)), W_HBM_SPEC],
        out_specs=pl.BlockSpec((tm, tn), lambda j, i: (i, j)),
        out_shape=jax.ShapeDtypeStruct((m, n_out), BF16),
        scratch_shapes=_weight_scratch(k, tn),
        compiler_params=_params("arbitrary", "arbitrary"),
        name="castproj",
    )(jnp.asarray(scales), h, ssq, lnw.reshape(k, 1), w)


def _gate_kernel(h_ref, ssq_ref, lnw_ref, w_hbm, o_ref, wbuf_ref, sem_ref, wbf_ref, *, e, tile0):
    _stage_weight(w_hbm, wbuf_ref, sem_ref, wbf_ref, e, tile0, lnw_ref)
    rinv = _rinv(ssq_ref, h_ref.shape[1])
    for rs in _row_blocks(h_ref.shape[0]):
        z = jnp.dot(h_ref[rs, :], wbf_ref[...], preferred_element_type=F32) * rinv[rs, :]
        o_ref[rs, :] = (z * (0.5 * jnp.tanh(0.5 * z) + 0.5)).astype(o_ref.dtype)


def _gate_proj(h, ssq, lnw, w, e, col0, n_out, cfg):
    m, k = h.shape
    tm, tn = min(cfg.tm, m), cfg.tn
    return pl.pallas_call(
        functools.partial(_gate_kernel, e=e, tile0=col0 // tn),
        grid=(n_out // tn, m // tm),
        in_specs=[pl.BlockSpec((tm, k), lambda j, i: (i, 0)),
                  pl.BlockSpec((tm, LANES), lambda j, i: (i, 0)),
                  pl.BlockSpec((k, 1), lambda j, i: (0, 0)), W_HBM_SPEC],
        out_specs=pl.BlockSpec((tm, tn), lambda j, i: (i, j)),
        out_shape=jax.ShapeDtypeStruct((m, n_out), BF16),
        scratch_shapes=_weight_scratch(k, tn),
        compiler_params=_params("arbitrary", "arbitrary"),
        name="gateproj",
    )(h, ssq, lnw.reshape(k, 1), w)


def _outproj_kernel(*refs, n_lhs, e, feed_next):
    y_refs = refs[:n_lhs]
    if feed_next:
        w_hbm, x_ref, o_ref, ob_ref, ssq_hbm, wbuf_ref, sem_ref, wbf_ref, ssq_acc_ref, ssq_sem = refs[n_lhs:]
    else:
        w_hbm, x_ref, o_ref, wbuf_ref, sem_ref, wbf_ref = refs[n_lhs:]
    _stage_weight(w_hbm, wbuf_ref, sem_ref, wbf_ref, e, 0)
    acc = x_ref[...]
    k0 = 0
    for y_ref in y_refs:
        kk = y_ref.shape[1]
        acc = acc + jnp.dot(y_ref[...], wbf_ref[k0:k0 + kk, :], preferred_element_type=F32)
        k0 += kk
    o_ref[...] = acc
    if not feed_next:
        return
    ob_ref[...] = acc.astype(ob_ref.dtype)
    tm = acc.shape[0]
    rows = pl.ds(pl.multiple_of(pl.program_id(1) * tm, tm), tm)
    part = _lane_partial_sumsq(acc)

    @pl.when(pl.program_id(0) == 0)
    def _():
        ssq_acc_ref[rows, :] = part

    @pl.when(pl.program_id(0) > 0)
    def _():
        ssq_acc_ref[rows, :] += part

    @pl.when(pl.program_id(0) == pl.num_programs(0) - 1)
    def _():
        copy = pltpu.make_async_copy(ssq_acc_ref.at[rows, :], ssq_hbm.at[rows, :], ssq_sem.at[0])
        copy.start()
        copy.wait()


def _outproj(ys, w, e, x, cfg, feed_next):
    m, n = x.shape
    k = w.shape[1]
    tm, tn = min(cfg.tm, m), cfg.tn
    tile = pl.BlockSpec((tm, tn), lambda j, i: (i, j))
    in_specs = [pl.BlockSpec((tm, y.shape[1]), lambda j, i: (i, 0)) for y in ys]
    in_specs += [W_HBM_SPEC, tile]
    out_specs, out_shape, scratch = [tile], [jax.ShapeDtypeStruct((m, n), F32)], _weight_scratch(k, tn)
    if feed_next:
        out_specs += [tile, pl.BlockSpec(memory_space=pl.ANY)]
        out_shape += [jax.ShapeDtypeStruct((m, n), BF16), jax.ShapeDtypeStruct((m, LANES), F32)]
        scratch += [pltpu.VMEM((m, LANES), F32), pltpu.SemaphoreType.DMA((1,))]
    outs = pl.pallas_call(
        functools.partial(_outproj_kernel, n_lhs=len(ys), e=e, feed_next=feed_next),
        grid=(n // tn, m // tm),
        in_specs=in_specs,
        out_specs=out_specs,
        out_shape=out_shape,
        scratch_shapes=scratch,
        compiler_params=_params("arbitrary", "arbitrary"),
        name="outproj",
    )(*ys, w, x)
    return tuple(outs) if feed_next else (outs[0], None, None)


def _attn_a_kernel(q_ref, k_ref, v_ref, g_ref, o_ref, qs_ref, v1_ref, s_ref, p_ref, m_ref, alpha_ref,
                   acc_ref, kmax_ref, *, tk, group, chunk):
    tq = q_ref.shape[0]
    s_len = k_ref.shape[0]
    nk = s_len // tk

    @pl.when(pl.program_id(1) == 0)
    def _():
        v1_ref[:, :LANES] = v_ref[...]
        v1_ref[:, LANES:] = jnp.ones((s_len, LANES), BF16)
        kf = k_ref[...].astype(F32)
        kmax_ref[0] = jnp.max(jnp.sum(kf * kf, axis=-1, keepdims=True))

    for g in range(group):
        qs_ref[g * tq:(g + 1) * tq, :] = q_ref[:, g * LANES:(g + 1) * LANES]
    rows = group * tq

    def scores(t):
        k = k_ref[t * tk:(t + 1) * tk, :]
        return lax.dot_general(qs_ref[...], k, (((1,), (1,)), ((), ())), preferred_element_type=F32)

    qf = qs_ref[...].astype(F32)
    bound = jnp.sqrt(jnp.sum(qf * qf, axis=-1, keepdims=True) * kmax_ref[0])

    def fixed_shift():
        acc_ref[...] = jnp.zeros((rows, 2 * LANES), F32)
        for t in range(nk):
            p = jnp.exp2(scores(t) - bound).astype(BF16)
            acc_ref[...] += jnp.dot(p, v1_ref[t * tk:(t + 1) * tk, :], preferred_element_type=F32)

    def running_max():
        m_ref[...] = jnp.full((rows, 1), NEG_INF, F32)
        acc_ref[...] = jnp.zeros((rows, 2 * LANES), F32)
        s_ref[0] = scores(0)
        for t in range(nk):
            if t + 1 < nk:
                s_ref[(t + 1) % 2] = scores(t + 1)
            for c in range(rows // chunk):
                rs = slice(c * chunk, (c + 1) * chunk)
                s = s_ref[t % 2, rs, :]
                m_old = m_ref[rs, :]
                m_new = jnp.maximum(m_old, jnp.max(s, axis=-1, keepdims=True))
                m_ref[rs, :] = m_new
                alpha_ref[rs, :] = jnp.exp2(m_old - m_new)
                p_ref[t % 2, rs, :] = jnp.exp2(s - m_new).astype(BF16)
            pv = jnp.dot(p_ref[t % 2], v1_ref[t * tk:(t + 1) * tk, :], preferred_element_type=F32)
            acc_ref[...] = alpha_ref[...] * acc_ref[...] + pv

    lax.cond(jnp.max(bound) <= A_SAFE_SHIFT, fixed_shift, running_max)
    o = acc_ref[:, :LANES] / acc_ref[:, LANES:]
    for g in range(group):
        sl = slice(g * LANES, (g + 1) * LANES)
        o_ref[:, sl] = (o[g * tq:(g + 1) * tq, :] * g_ref[:, sl]).astype(o_ref.dtype)


def _attn_a(q, k, v, gate, cfg):
    (q_arr, q_col0), (k_arr, k_col0), (v_arr, v_col0) = q, k, v
    s_len = q_arr.shape[0]
    group = cfg.a_heads // cfg.a_kv_heads
    gw = group * LANES
    tq = min(cfg.tq_a, s_len)
    tk = min(cfg.tk_a, s_len)
    return pl.pallas_call(
        functools.partial(_attn_a_kernel, tk=tk, group=group, chunk=cfg.chunk_a),
        grid=(cfg.a_kv_heads, s_len // tq),
        scratch_shapes=[pltpu.VMEM((group * tq, LANES), BF16),
                        pltpu.VMEM((s_len, 2 * LANES), BF16),
                        pltpu.VMEM((2, group * tq, tk), F32),
                        pltpu.VMEM((2, group * tq, tk), BF16),
                        pltpu.VMEM((group * tq, 1), F32),
                        pltpu.VMEM((group * tq, 1), F32),
                        pltpu.VMEM((group * tq, 2 * LANES), F32),
                        pltpu.SMEM((1,), F32)],
        in_specs=[pl.BlockSpec((tq, gw), lambda kv, i: (i, q_col0 // gw + kv)),
                  pl.BlockSpec((s_len, LANES), lambda kv, i: (0, k_col0 // LANES + kv)),
                  pl.BlockSpec((s_len, LANES), lambda kv, i: (0, v_col0 // LANES + kv)),
                  pl.BlockSpec((tq, gw), lambda kv, i: (i, kv))],
        out_specs=pl.BlockSpec((tq, gw), lambda kv, i: (i, kv)),
        out_shape=jax.ShapeDtypeStruct((s_len, cfg.a_heads * LANES), BF16),
        compiler_params=_params("parallel", "arbitrary"),
        name="attn_global",
    )(q_arr, k_arr, v_arr, gate)


def _natten_geometry(rows):
    types, block_type = [], []
    for b in range(rows // NA_QB_ROWS):
        r = b * NA_QB_ROWS + np.arange(NA_QB_ROWS)
        u0 = int(np.clip(r[0] - NA_ROWS // 2, 0, rows - NA_UNION_ROWS))
        rs = np.clip(r - NA_ROWS // 2, 0, rows - NA_ROWS)
        assert u0 <= rs.min() and rs.max() + NA_ROWS <= u0 + NA_UNION_ROWS
        kr = u0 + np.arange(NA_UNION_ROWS)
        vr = (kr[None, :] >= rs[:, None]) & (kr[None, :] < rs[:, None] + NA_ROWS)
        dr = np.where(vr, kr[None, :] - r[:, None] + (NA_ROWS - 1), 0)
        for t, (dr_t, vr_t) in enumerate(types):
            if np.array_equal(dr, dr_t) and np.array_equal(vr, vr_t):
                block_type.append(t)
                break
        else:
            block_type.append(len(types))
            types.append((dr, vr))
    return types, block_type


def _build_natten_tables(rpb_ref, tbl_ref, types):
    c = lax.broadcasted_iota(jnp.int32, (GRID_W, LANES), 0)
    lane = lax.broadcasted_iota(jnp.int32, (GRID_W, LANES), 1)
    first = lane < GRID_W
    kc = jnp.where(first, lane, lane - GRID_W)
    cs = jnp.clip(c - NA_COLS // 2, 0, GRID_W - NA_COLS)
    vc = (kc >= cs) & (kc < cs + NA_COLS)
    base_shift = LANES - (NA_COLS - 1)
    for t, (dr, vr) in enumerate(types):
        for i in range(NA_QB_ROWS):
            for kp in range(NA_UNION_ROWS // 2):
                halves = []
                for half in range(2):
                    ku = 2 * kp + half
                    if vr[i, ku]:
                        row = jnp.broadcast_to(rpb_ref[int(dr[i, ku]):int(dr[i, ku]) + 1, :],
                                               (GRID_W, LANES))
                        halves.append(pltpu.roll(row, (base_shift + half * GRID_W) % LANES, 1,
                                                 stride=1, stride_axis=0))
                    else:
                        halves.append(None)
                lo, hi = halves
                if lo is None and hi is None:
                    tile = jnp.full((GRID_W, LANES), NEG_INF, F32)
                else:
                    ok = vc
                    if lo is None:
                        val, ok = hi, vc & ~first
                    elif hi is None:
                        val, ok = lo, vc & first
                    else:
                        val = jnp.where(first, lo, hi)
                    tile = jnp.where(ok, val * LOG2E, NEG_INF)
                tbl_ref[t, i * GRID_W:(i + 1) * GRID_W, kp * LANES:(kp + 1) * LANES] = tile


def _natten_kernel(q_ref, k_ref, v_ref, g_ref, rpb_ref, o_ref, tbl_ref, v1_ref, *, rows):
    qb = NA_QB_ROWS * GRID_W
    span = NA_UNION_ROWS * GRID_W
    nb_step = q_ref.shape[0] // qb
    types, block_type = _natten_geometry(rows)
    common = max(set(block_type), key=block_type.count)
    step = pl.program_id(1)

    @pl.when(step == 0)
    def _():
        _build_natten_tables(rpb_ref, tbl_ref, types)
        v1_ref[:, :LANES] = v_ref[...]
        v1_ref[:, LANES:] = jnp.ones((v_ref.shape[0], LANES), BF16)

    for b in range(nb_step):
        blk = step * nb_step + b
        u0 = jnp.clip(blk * NA_QB_ROWS - NA_ROWS // 2, 0, rows - NA_UNION_ROWS)
        typ = common
        for bb, t in enumerate(block_type):
            if t != common:
                typ = jnp.where(blk == bb, t, typ)
        koff = pl.multiple_of(u0 * GRID_W, GRID_W)
        qs = slice(b * qb, (b + 1) * qb)
        k = k_ref[pl.ds(koff, span), :]
        v1 = v1_ref[pl.ds(koff, span), :]
        s = lax.dot_general(q_ref[qs, :], k, (((1,), (1,)), ((), ())), preferred_element_type=F32)
        s = s + tbl_ref[typ]
        m = jnp.max(s, axis=-1, keepdims=True)
        pv = jnp.dot(jnp.exp2(s - m).astype(BF16), v1, preferred_element_type=F32)
        o_ref[qs, :] = (pv[:, :LANES] / pv[:, LANES:] * g_ref[qs, :]).astype(o_ref.dtype)


def _natten(q, k, v, gate, gate_col0, rpb, cfg):
    (q_arr, q_col0), (k_arr, k_col0), (v_arr, v_col0) = q, k, v
    s_len = q_arr.shape[0]
    rows = s_len // GRID_W
    tq = min(cfg.tq_b, s_len)
    qb = NA_QB_ROWS * GRID_W
    span = NA_UNION_ROWS * GRID_W
    assert rows % NA_QB_ROWS == 0 and rows >= NA_UNION_ROWS and tq % qb == 0
    n_types = len(_natten_geometry(rows)[0])
    nh, nr, nc = rpb.shape
    rpb_pad = jnp.pad(rpb.astype(F32), ((0, 0), (0, 2 * NA_ROWS - nr), (0, LANES - nc)))
    return pl.pallas_call(
        functools.partial(_natten_kernel, rows=rows),
        grid=(cfg.b_heads, s_len // tq),
        in_specs=[pl.BlockSpec((tq, LANES), lambda h, i: (i, q_col0 // LANES + h)),
                  pl.BlockSpec((s_len, LANES), lambda h, i: (0, k_col0 // LANES + h)),
                  pl.BlockSpec((s_len, LANES), lambda h, i: (0, v_col0 // LANES + h)),
                  pl.BlockSpec((tq, LANES), lambda h, i: (i, gate_col0 // LANES + h)),
                  pl.BlockSpec((None, 2 * NA_ROWS, LANES), lambda h, i: (h, 0, 0))],
        out_specs=pl.BlockSpec((tq, LANES), lambda h, i: (i, h)),
        out_shape=jax.ShapeDtypeStruct((s_len, cfg.b_heads * LANES), BF16),
        scratch_shapes=[pltpu.VMEM((n_types, qb, span), F32),
                        pltpu.VMEM((s_len, 2 * LANES), BF16)],
        compiler_params=_params("parallel", "arbitrary"),
        name="attn_neighbourhood",
    )(q_arr, k_arr, v_arr, gate, rpb_pad)


def _attn_c_kernel(slope_ref, sink_ref, q_ref, k_ref, v_ref, g_ref, o_ref, v1_ref, bias_ref, *, group, sub):
    tq = q_ref.shape[0]
    s_len = k_ref.shape[0]
    span = sub + 2 * C_WINDOW
    kv = pl.program_id(0)

    n_place = 3

    @pl.when(pl.program_id(1) == 0)
    def _():
        v1_ref[:, :LANES] = v_ref[...]
        v1_ref[:, LANES:] = jnp.ones((s_len, LANES), BF16)
        row = lax.broadcasted_iota(jnp.int32, (sub, span), 0)
        col = lax.broadcasted_iota(jnp.int32, (sub, span), 1)
        for place in range(n_place):
            dist = jnp.abs(row - col + place * C_WINDOW)
            pen = jnp.where(dist <= C_WINDOW, dist.astype(F32), -NEG_INF)
            for g in range(group):
                bias_ref[place * group + g] = -(slope_ref[kv * group + g] * LOG2E) * pen

    sinks = [sink_ref[kv * group + g] * LOG2E for g in range(group)]
    for b in range(tq // sub):
        t0 = pl.program_id(1) * tq + b * sub
        start = pl.multiple_of(jnp.clip(t0 - C_WINDOW, 0, s_len - span), C_WINDOW)
        place = (t0 - start) // C_WINDOW
        k = k_ref[pl.ds(start, span), :]
        v1 = v1_ref[pl.ds(start, span), :]
        rows = slice(b * sub, (b + 1) * sub)
        q = jnp.concatenate([q_ref[rows, g * LANES:(g + 1) * LANES] for g in range(group)], axis=0)
        s = lax.dot_general(q, k, (((1,), (1,)), ((), ())), preferred_element_type=F32)
        ps, ms = [], []
        for g in range(group):
            sg = s[g * sub:(g + 1) * sub, :] + bias_ref[place * group + g]
            m = jnp.maximum(jnp.max(sg, axis=-1, keepdims=True), sinks[g])
            ps.append(jnp.exp2(sg - m).astype(BF16))
            ms.append(m)
        pv = jnp.dot(jnp.concatenate(ps, axis=0), v1, preferred_element_type=F32)
        for g in range(group):
            sl = slice(g * LANES, (g + 1) * LANES)
            pg = pv[g * sub:(g + 1) * sub, :]
            l = pg[:, LANES:] + jnp.exp2(sinks[g] - ms[g])
            o_ref[rows, sl] = (pg[:, :LANES] / l * g_ref[rows, sl]).astype(o_ref.dtype)


def _attn_c(q, k, v, gate, slopes, sink, cfg):
    (q_arr, q_col0), (k_arr, k_col0), (v_arr, v_col0) = q, k, v
    s_len = q_arr.shape[0]
    group = cfg.c_heads // cfg.c_kv_heads
    gw = group * LANES
    tq, sub = cfg.tq_c, cfg.sub_c
    assert s_len >= sub + 2 * C_WINDOW and s_len % tq == 0 and tq % sub == 0 and sub % C_WINDOW == 0
    smem = pl.BlockSpec(memory_space=pltpu.SMEM)
    return pl.pallas_call(
        functools.partial(_attn_c_kernel, group=group, sub=sub),
        grid=(cfg.c_kv_heads, s_len // tq),
        scratch_shapes=[pltpu.VMEM((s_len, 2 * LANES), BF16),
                        pltpu.VMEM((3 * group, sub, sub + 2 * C_WINDOW), F32)],
        in_specs=[smem, smem,
                  pl.BlockSpec((tq, gw), lambda kv, i: (i, q_col0 // gw + kv)),
                  pl.BlockSpec((s_len, LANES), lambda kv, i: (0, k_col0 // LANES + kv)),
                  pl.BlockSpec((s_len, LANES), lambda kv, i: (0, v_col0 // LANES + kv)),
                  pl.BlockSpec((tq, gw), lambda kv, i: (i, kv))],
        out_specs=pl.BlockSpec((tq, gw), lambda kv, i: (i, kv)),
        out_shape=jax.ShapeDtypeStruct((s_len, cfg.c_heads * LANES), BF16),
        compiler_params=_params("parallel", "arbitrary"),
        name="attn_window",
    )(slopes, sink, q_arr, k_arr, v_arr, gate)


def _rope_tables(s_len):
    half = LANES // 2
    quarter = half // 2
    inv = jnp.exp(-math.log(ROPE_THETA) * jnp.arange(0, half, 2, dtype=F32) / half)
    pos = jnp.arange(s_len, dtype=jnp.int32)
    rows = (pos // GRID_W).astype(F32)
    cols = (pos % GRID_W).astype(F32)
    ang = jnp.concatenate([rows[:, None] * inv[None, :]] * 2 + [cols[:, None] * inv[None, :]] * 2, axis=1)
    first = (np.arange(LANES) % half) < quarter
    cos, sin = jnp.cos(ang), jnp.sin(ang)
    return cos, jnp.where(first[None], -sin, 0.0), jnp.where(first[None], 0.0, sin)


def _forward(x, norm_w, w_in_ab, w_out_ab, q_norm_a, k_norm_a, rpb_b, w_in_c, w_out_c, sink_c,
             final_norm_w, cfg):
    bsz, s_len, d = x.shape
    assert bsz == 1 and s_len % GRID_W == 0
    depth = norm_w.shape[0]
    scale = LANES ** -0.5 * LOG2E
    a_q, a_kv, b_w = cfg.a_heads * LANES, cfg.a_kv_heads * LANES, cfg.b_heads * LANES
    c_q, c_kv = cfg.c_heads * LANES, cfg.c_kv_heads * LANES
    rope = _rope_tables(s_len)
    slopes = jnp.exp2(-8.0 * jnp.arange(1, cfg.c_heads + 1, dtype=F32) / cfg.c_heads)
    xs = x.reshape(s_len, d)
    h, ssq = _prep(xs, cfg.t_norm)
    for layer in range(depth):
        lnw = norm_w[layer]
        if layer % 2 == 0:
            e = layer // 2
            n_rope = a_q + a_kv
            zr = _rope_proj(h, ssq, lnw, w_in_ab, e, a_q, a_kv, scale, rope, q_norm_a[e], k_norm_a[e], cfg)
            zc = _cast_proj(h, ssq, lnw, w_in_ab, e, n_rope, ((a_kv, 1.0), (b_w, scale), (2 * b_w, 1.0)),
                            cfg)
            gate = _gate_proj(h, ssq, lnw, w_in_ab, e, n_rope + a_kv + 3 * b_w, a_q + b_w, cfg)
            ya = _attn_a((zr, 0), (zr, a_q), (zc, 0), gate, cfg)
            yb = _natten((zc, a_kv), (zc, a_kv + b_w), (zc, a_kv + 2 * b_w), gate, a_q, rpb_b[e], cfg)
            xs, h, ssq = _outproj([ya, yb], w_out_ab, e, xs, cfg, layer + 1 < depth)
        else:
            o = layer // 2
            zc = _cast_proj(h, ssq, lnw, w_in_c, o, 0, ((c_q, scale), (2 * c_kv, 1.0)), cfg)
            gate = _gate_proj(h, ssq, lnw, w_in_c, o, c_q + 2 * c_kv, c_q, cfg)
            y = _attn_c((zc, 0), (zc, c_q), (zc, c_q + c_kv), gate, slopes, sink_c[o].astype(F32), cfg)
            xs, h, ssq = _outproj([y], w_out_c, o, xs, cfg, layer + 1 < depth)
    out = _rmsnorm(xs, final_norm_w, x.dtype, cfg.t_norm)
    return out.reshape(bsz, s_len, d)


def kernel(x, norm_w, w_in_ab, w_out_ab, q_norm_a, k_norm_a, rpb_b, w_in_c, w_out_c, sink_c, final_norm_w):
    return _forward(x, norm_w, w_in_ab, w_out_ab, q_norm_a, k_norm_a, rpb_b, w_in_c, w_out_c, sink_c,
                    final_norm_w, Config())
```

```python
import functools
import math
from typing import NamedTuple

import jax
import jax.numpy as jnp
import numpy as np
from jax import lax
from jax.experimental import pallas as pl
from jax.experimental.pallas import tpu as pltpu

F32 = jnp.float32
BF16 = jnp.bfloat16

LANES = 128
NORM_EPS = 1e-6
NEG_INF = -1e30
LOG2E = math.log2(math.e)
ROPE_THETA = 10000.0
GRID_W = 64
NA_ROWS = 8
NA_COLS = 16
NA_QB_ROWS = 4
NA_UNION_ROWS = 12
C_WINDOW = 128
PROJ_SPLIT = 8
CAST_SPLIT = 4
A_SAFE_SHIFT = 60.0
VMEM_LIMIT_BYTES = 56 * 1024 * 1024


class Config(NamedTuple):
    a_heads: int = 16
    a_kv_heads: int = 4
    b_heads: int = 16
    c_heads: int = 32
    c_kv_heads: int = 8
    tm: int = 1024
    tn: int = 512
    tq_a: int = 256
    tk_a: int = 1024
    chunk_a: int = 16
    tq_c: int = 2048
    sub_c: int = 128
    tq_b: int = 4096
    t_norm: int = 256


def _params(*sem):
    return pltpu.CompilerParams(dimension_semantics=sem, vmem_limit_bytes=VMEM_LIMIT_BYTES)


def _rmsnorm_kernel(x_ref, w_ref, o_ref):
    x = x_ref[...]
    ms = jnp.mean(x * x, axis=-1, keepdims=True)
    o_ref[...] = (x * lax.rsqrt(ms + NORM_EPS) * w_ref[...]).astype(o_ref.dtype)


def _rmsnorm(x, w, out_dtype, t):
    m, d = x.shape
    return pl.pallas_call(
        _rmsnorm_kernel,
        grid=(m // t,),
        in_specs=[pl.BlockSpec((t, d), lambda i: (i, 0)),
                  pl.BlockSpec((1, d), lambda i: (0, 0))],
        out_specs=pl.BlockSpec((t, d), lambda i: (i, 0)),
        out_shape=jax.ShapeDtypeStruct((m, d), out_dtype),
        compiler_params=_params("parallel"),
        name="rmsnorm",
    )(x, w.reshape(1, d))


def _lane_partial_sumsq(x):
    sq = x * x
    acc = sq[:, :LANES]
    for c in range(1, x.shape[1] // LANES):
        acc = acc + sq[:, c * LANES:(c + 1) * LANES]
    return acc


def _prep_kernel(x_ref, xb_ref, ssq_ref):
    x = x_ref[...]
    xb_ref[...] = x.astype(xb_ref.dtype)
    ssq_ref[...] = _lane_partial_sumsq(x)


def _prep(x, t):
    m, d = x.shape
    return pl.pallas_call(
        _prep_kernel,
        grid=(m // t,),
        in_specs=[pl.BlockSpec((t, d), lambda i: (i, 0))],
        out_specs=[pl.BlockSpec((t, d), lambda i: (i, 0)), pl.BlockSpec((t, LANES), lambda i: (i, 0))],
        out_shape=[jax.ShapeDtypeStruct((m, d), BF16), jax.ShapeDtypeStruct((m, LANES), F32)],
        compiler_params=_params("parallel"),
        name="prep",
    )(x)


def _weight_copy(w_hbm, wbuf_ref, sem_ref, e, tile, slot):
    tn = wbuf_ref.shape[2]
    cols = pl.ds(pl.multiple_of(tile * tn, tn), tn)
    return pltpu.make_async_copy(w_hbm.at[e, :, cols], wbuf_ref.at[slot], sem_ref.at[slot])


def _stage_weight(w_hbm, wbuf_ref, sem_ref, wbf_ref, e, tile0, row_scale_ref=None):
    j, nj = pl.program_id(0), pl.num_programs(0)

    @pl.when(pl.program_id(1) == 0)
    def _():
        slot = j % 2

        @pl.when(j == 0)
        def _():
            _weight_copy(w_hbm, wbuf_ref, sem_ref, e, tile0, 0).start()

        @pl.when(j + 1 < nj)
        def _():
            _weight_copy(w_hbm, wbuf_ref, sem_ref, e, tile0 + j + 1, 1 - slot).start()

        _weight_copy(w_hbm, wbuf_ref, sem_ref, e, tile0 + j, slot).wait()
        w = wbuf_ref[slot]
        wbf_ref[...] = (w if row_scale_ref is None else w * row_scale_ref[...]).astype(BF16)


def _weight_scratch(k, tn):
    return [pltpu.VMEM((2, k, tn), F32), pltpu.SemaphoreType.DMA((2,)), pltpu.VMEM((k, tn), BF16)]


W_HBM_SPEC = pl.BlockSpec(memory_space=pl.ANY)


def _norm_rope(z, w, cos, sin_lo, sin_hi):
    ms = jnp.mean(z * z, axis=-1, keepdims=True)
    y = z * lax.rsqrt(ms + NORM_EPS) * w
    quarter = LANES // 4
    return (y * cos + pltpu.roll(y, LANES - quarter, 1) * sin_lo
            + pltpu.roll(y, quarter, 1) * sin_hi)


def _rinv(ssq_ref, d):
    return lax.rsqrt(jnp.sum(ssq_ref[...], axis=-1, keepdims=True) * (1.0 / d) + NORM_EPS)


def _row_blocks(tm, split=PROJ_SPLIT):
    step = tm // split
    return [slice(r * step, (r + 1) * step) for r in range(split)]


def _rope_proj_kernel(h_ref, ssq_ref, lnw_ref, w_hbm, cos_ref, slo_ref, shi_ref, qn_ref, kn_ref, o_ref,
                      wbuf_ref, sem_ref, wbf_ref, *, e, q_tiles, q_scale):
    _stage_weight(w_hbm, wbuf_ref, sem_ref, wbf_ref, e, 0, lnw_ref)
    nw = jnp.where(pl.program_id(0) < q_tiles, qn_ref[...] * q_scale, kn_ref[...])
    rinv = _rinv(ssq_ref, h_ref.shape[1])
    for rs in _row_blocks(h_ref.shape[0]):
        acc = jnp.dot(h_ref[rs, :], wbf_ref[...], preferred_element_type=F32) * rinv[rs, :]
        cos, slo, shi = cos_ref[rs, :], slo_ref[rs, :], shi_ref[rs, :]
        for hh in range(acc.shape[1] // LANES):
            sl = slice(hh * LANES, (hh + 1) * LANES)
            o_ref[rs, sl] = _norm_rope(acc[:, sl], nw, cos, slo, shi).astype(o_ref.dtype)


def _rope_proj(h, ssq, lnw, w, e, n_q, n_k, q_scale, rope, qn, kn, cfg):
    m, k = h.shape
    tm, tn = min(cfg.tm, m), cfg.tn
    row = lambda j, i: (i, 0)
    vec = pl.BlockSpec((1, LANES), lambda j, i: (0, 0))
    return pl.pallas_call(
        functools.partial(_rope_proj_kernel, e=e, q_tiles=n_q // tn, q_scale=q_scale),
        grid=((n_q + n_k) // tn, m // tm),
        in_specs=[pl.BlockSpec((tm, k), row), pl.BlockSpec((tm, LANES), row),
                  pl.BlockSpec((k, 1), lambda j, i: (0, 0)), W_HBM_SPEC,
                  pl.BlockSpec((tm, LANES), row), pl.BlockSpec((tm, LANES), row),
                  pl.BlockSpec((tm, LANES), row), vec, vec],
        out_specs=pl.BlockSpec((tm, tn), lambda j, i: (i, j)),
        out_shape=jax.ShapeDtypeStruct((m, n_q + n_k), BF16),
        scratch_shapes=_weight_scratch(k, tn),
        compiler_params=_params("arbitrary", "arbitrary"),
        name="ropeproj",
    )(h, ssq, lnw.reshape(k, 1), w, *rope, qn.reshape(1, LANES), kn.reshape(1, LANES))


def _cast_proj_kernel(scale_ref, h_ref, ssq_ref, lnw_ref, w_hbm, o_ref, wbuf_ref, sem_ref, wbf_ref, *,
                      e, tile0):
    _stage_weight(w_hbm, wbuf_ref, sem_ref, wbf_ref, e, tile0, lnw_ref)
    scale = _rinv(ssq_ref, h_ref.shape[1]) * scale_ref[pl.program_id(0)]
    for rs in _row_blocks(h_ref.shape[0], CAST_SPLIT):
        acc = jnp.dot(h_ref[rs, :], wbf_ref[...], preferred_element_type=F32)
        o_ref[rs, :] = (acc * scale[rs, :]).astype(o_ref.dtype)


def _cast_proj(h, ssq, lnw, w, e, col0, widths_scales, cfg):
    m, k = h.shape
    tm, tn = min(cfg.tm, m), cfg.tn
    scales = np.concatenate([np.full(width // tn, scale, np.float32) for width, scale in widths_scales])
    n_out = tn * len(scales)
    return pl.pallas_call(
        functools.partial(_cast_proj_kernel, e=e, tile0=col0 // tn),
        grid=(n_out // tn, m // tm),
        in_specs=[pl.BlockSpec(memory_space=pltpu.SMEM),
                  pl.BlockSpec((tm, k), lambda j, i: (i, 0)),
                  pl.BlockSpec((tm, LANES), lambda j, i: (i, 0)),
                  pl.BlockSpec((k, 1), lambda j, i: (0, 0)), W_HBM_SPEC],
        out_specs=pl.BlockSpec((tm, tn), lambda j, i: (i, j)),
        out_shape=jax.ShapeDtypeStruct((m, n_out), BF16),
        scratch_shapes=_weight_scratch(k, tn),
        compiler_params=_params("arbitrary", "arbitrary"),
        name="castproj",
    )(jnp.asarray(scales), h, ssq, lnw.reshape(k, 1), w)


def _gate_kernel(h_ref, ssq_ref, lnw_ref, w_hbm, o_ref, wbuf_ref, sem_ref, wbf_ref, *, e, tile0):
    _stage_weight(w_hbm, wbuf_ref, sem_ref, wbf_ref, e, tile0, lnw_ref)
    rinv = _rinv(ssq_ref, h_ref.shape[1])
    for rs in _row_blocks(h_ref.shape[0]):
        z = jnp.dot(h_ref[rs, :], wbf_ref[...], preferred_element_type=F32) * rinv[rs, :]
        o_ref[rs, :] = (z * (0.5 * jnp.tanh(0.5 * z) + 0.5)).astype(o_ref.dtype)


def _gate_proj(h, ssq, lnw, w, e, col0, n_out, cfg):
    m, k = h.shape
    tm, tn = min(cfg.tm, m), cfg.tn
    return pl.pallas_call(
        functools.partial(_gate_kernel, e=e, tile0=col0 // tn),
        grid=(n_out // tn, m // tm),
        in_specs=[pl.BlockSpec((tm, k), lambda j, i: (i, 0)),
                  pl.BlockSpec((tm, LANES), lambda j, i: (i, 0)),
                  pl.BlockSpec((k, 1), lambda j, i: (0, 0)), W_HBM_SPEC],
        out_specs=pl.BlockSpec((tm, tn), lambda j, i: (i, j)),
        out_shape=jax.ShapeDtypeStruct((m, n_out), BF16),
        scratch_shapes=_weight_scratch(k, tn),
        compiler_params=_params("arbitrary", "arbitrary"),
        name="gateproj",
    )(h, ssq, lnw.reshape(k, 1), w)


def _outproj_kernel(*refs, n_lhs, e, feed_next):
    y_refs = refs[:n_lhs]
    if feed_next:
        w_hbm, x_ref, o_ref, ob_ref, ssq_hbm, wbuf_ref, sem_ref, wbf_ref, ssq_acc_ref, ssq_sem = refs[n_lhs:]
    else:
        w_hbm, x_ref, o_ref, wbuf_ref, sem_ref, wbf_ref = refs[n_lhs:]
    _stage_weight(w_hbm, wbuf_ref, sem_ref, wbf_ref, e, 0)
    acc = x_ref[...]
    k0 = 0
    for y_ref in y_refs:
        kk = y_ref.shape[1]
        acc = acc + jnp.dot(y_ref[...], wbf_ref[k0:k0 + kk, :], preferred_element_type=F32)
        k0 += kk
    o_ref[...] = acc
    if not feed_next:
        return
    ob_ref[...] = acc.astype(ob_ref.dtype)
    tm = acc.shape[0]
    rows = pl.ds(pl.multiple_of(pl.program_id(1) * tm, tm), tm)
    part = _lane_partial_sumsq(acc)

    @pl.when(pl.program_id(0) == 0)
    def _():
        ssq_acc_ref[rows, :] = part

    @pl.when(pl.program_id(0) > 0)
    def _():
        ssq_acc_ref[rows, :] += part

    @pl.when(pl.program_id(0) == pl.num_programs(0) - 1)
    def _():
        copy = pltpu.make_async_copy(ssq_acc_ref.at[rows, :], ssq_hbm.at[rows, :], ssq_sem.at[0])
        copy.start()
        copy.wait()


def _outproj(ys, w, e, x, cfg, feed_next):
    m, n = x.shape
    k = w.shape[1]
    tm, tn = min(cfg.tm, m), cfg.tn
    tile = pl.BlockSpec((tm, tn), lambda j, i: (i, j))
    in_specs = [pl.BlockSpec((tm, y.shape[1]), lambda j, i: (i, 0)) for y in ys]
    in_specs += [W_HBM_SPEC, tile]
    out_specs, out_shape, scratch = [tile], [jax.ShapeDtypeStruct((m, n), F32)], _weight_scratch(k, tn)
    if feed_next:
        out_specs += [tile, pl.BlockSpec(memory_space=pl.ANY)]
        out_shape += [jax.ShapeDtypeStruct((m, n), BF16), jax.ShapeDtypeStruct((m, LANES), F32)]
        scratch += [pltpu.VMEM((m, LANES), F32), pltpu.SemaphoreType.DMA((1,))]
    outs = pl.pallas_call(
        functools.partial(_outproj_kernel, n_lhs=len(ys), e=e, feed_next=feed_next),
        grid=(n // tn, m // tm),
        in_specs=in_specs,
        out_specs=out_specs,
        out_shape=out_shape,
        scratch_shapes=scratch,
        compiler_params=_params("arbitrary", "arbitrary"),
        name="outproj",
    )(*ys, w, x)
    return tuple(outs) if feed_next else (outs[0], None, None)


def _attn_a_kernel(q_ref, k_ref, v_ref, g_ref, o_ref, qs_ref, v1_ref, s_ref, p_ref, m_ref, alpha_ref,
                   acc_ref, kmax_ref, *, tk, group, chunk):
    tq = q_ref.shape[0]
    s_len = k_ref.shape[0]
    nk = s_len // tk

    @pl.when(pl.program_id(1) == 0)
    def _():
        v1_ref[:, :LANES] = v_ref[...]
        v1_ref[:, LANES:] = jnp.ones((s_len, LANES), BF16)
        kf = k_ref[...].astype(F32)
        kmax_ref[0] = jnp.max(jnp.sum(kf * kf, axis=-1, keepdims=True))

    for g in range(group):
        qs_ref[g * tq:(g + 1) * tq, :] = q_ref[:, g * LANES:(g + 1) * LANES]
    rows = group * tq

    def scores(t):
        k = k_ref[t * tk:(t + 1) * tk, :]
        return lax.dot_general(qs_ref[...], k, (((1,), (1,)), ((), ())), preferred_element_type=F32)

    qf = qs_ref[...].astype(F32)
    bound = jnp.sqrt(jnp.sum(qf * qf, axis=-1, keepdims=True) * kmax_ref[0])

    def fixed_shift():
        acc_ref[...] = jnp.zeros((rows, 2 * LANES), F32)
        for t in range(nk):
            p = jnp.exp2(scores(t) - bound).astype(BF16)
            acc_ref[...] += jnp.dot(p, v1_ref[t * tk:(t + 1) * tk, :], preferred_element_type=F32)

    def running_max():
        m_ref[...] = jnp.full((rows, 1), NEG_INF, F32)
        acc_ref[...] = jnp.zeros((rows, 2 * LANES), F32)
        s_ref[0] = scores(0)
        for t in range(nk):
            if t + 1 < nk:
                s_ref[(t + 1) % 2] = scores(t + 1)
            for c in range(rows // chunk):
                rs = slice(c * chunk, (c + 1) * chunk)
                s = s_ref[t % 2, rs, :]
                m_old = m_ref[rs, :]
                m_new = jnp.maximum(m_old, jnp.max(s, axis=-1, keepdims=True))
                m_ref[rs, :] = m_new
                alpha_ref[rs, :] = jnp.exp2(m_old - m_new)
                p_ref[t % 2, rs, :] = jnp.exp2(s - m_new).astype(BF16)
            pv = jnp.dot(p_ref[t % 2], v1_ref[t * tk:(t + 1) * tk, :], preferred_element_type=F32)
            acc_ref[...] = alpha_ref[...] * acc_ref[...] + pv

    lax.cond(jnp.max(bound) <= A_SAFE_SHIFT, fixed_shift, running_max)
    o = acc_ref[:, :LANES] / acc_ref[:, LANES:]
    for g in range(group):
        sl = slice(g * LANES, (g + 1) * LANES)
        o_ref[:, sl] = (o[g * tq:(g + 1) * tq, :] * g_ref[:, sl]).astype(o_ref.dtype)


def _attn_a(q, k, v, gate, cfg):
    (q_arr, q_col0), (k_arr, k_col0), (v_arr, v_col0) = q, k, v
    s_len = q_arr.shape[0]
    group = cfg.a_heads // cfg.a_kv_heads
    gw = group * LANES
    tq = min(cfg.tq_a, s_len)
    tk = min(cfg.tk_a, s_len)
    return pl.pallas_call(
        functools.partial(_attn_a_kernel, tk=tk, group=group, chunk=cfg.chunk_a),
        grid=(cfg.a_kv_heads, s_len // tq),
        scratch_shapes=[pltpu.VMEM((group * tq, LANES), BF16),
                        pltpu.VMEM((s_len, 2 * LANES), BF16),
                        pltpu.VMEM((2, group * tq, tk), F32),
                        pltpu.VMEM((2, group * tq, tk), BF16),
                        pltpu.VMEM((group * tq, 1), F32),
                        pltpu.VMEM((group * tq, 1), F32),
                        pltpu.VMEM((group * tq, 2 * LANES), F32),
                        pltpu.SMEM((1,), F32)],
        in_specs=[pl.BlockSpec((tq, gw), lambda kv, i: (i, q_col0 // gw + kv)),
                  pl.BlockSpec((s_len, LANES), lambda kv, i: (0, k_col0 // LANES + kv)),
                  pl.BlockSpec((s_len, LANES), lambda kv, i: (0, v_col0 // LANES + kv)),
                  pl.BlockSpec((tq, gw), lambda kv, i: (i, kv))],
        out_specs=pl.BlockSpec((tq, gw), lambda kv, i: (i, kv)),
        out_shape=jax.ShapeDtypeStruct((s_len, cfg.a_heads * LANES), BF16),
        compiler_params=_params("parallel", "arbitrary"),
        name="attn_global",
    )(q_arr, k_arr, v_arr, gate)


def _natten_geometry(rows):
    types, block_type = [], []
    for b in range(rows // NA_QB_ROWS):
        r = b * NA_QB_ROWS + np.arange(NA_QB_ROWS)
        u0 = int(np.clip(r[0] - NA_ROWS // 2, 0, rows - NA_UNION_ROWS))
        rs = np.clip(r - NA_ROWS // 2, 0, rows - NA_ROWS)
        assert u0 <= rs.min() and rs.max() + NA_ROWS <= u0 + NA_UNION_ROWS
        kr = u0 + np.arange(NA_UNION_ROWS)
        vr = (kr[None, :] >= rs[:, None]) & (kr[None, :] < rs[:, None] + NA_ROWS)
        dr = np.where(vr, kr[None, :] - r[:, None] + (NA_ROWS - 1), 0)
        for t, (dr_t, vr_t) in enumerate(types):
            if np.array_equal(dr, dr_t) and np.array_equal(vr, vr_t):
                block_type.append(t)
                break
        else:
            block_type.append(len(types))
            types.append((dr, vr))
    return types, block_type


def _build_natten_tables(rpb_ref, tbl_ref, types):
    c = lax.broadcasted_iota(jnp.int32, (GRID_W, LANES), 0)
    lane = lax.broadcasted_iota(jnp.int32, (GRID_W, LANES), 1)
    first = lane < GRID_W
    kc = jnp.where(first, lane, lane - GRID_W)
    cs = jnp.clip(c - NA_COLS // 2, 0, GRID_W - NA_COLS)
    vc = (kc >= cs) & (kc < cs + NA_COLS)
    base_shift = LANES - (NA_COLS - 1)
    for t, (dr, vr) in enumerate(types):
        for i in range(NA_QB_ROWS):
            for kp in range(NA_UNION_ROWS // 2):
                halves = []
                for half in range(2):
                    ku = 2 * kp + half
                    if vr[i, ku]:
                        row = jnp.broadcast_to(rpb_ref[int(dr[i, ku]):int(dr[i, ku]) + 1, :],
                                               (GRID_W, LANES))
                        halves.append(pltpu.roll(row, (base_shift + half * GRID_W) % LANES, 1,
                                                 stride=1, stride_axis=0))
                    else:
                        halves.append(None)
                lo, hi = halves
                if lo is None and hi is None:
                    tile = jnp.full((GRID_W, LANES), NEG_INF, F32)
                else:
                    ok = vc
                    if lo is None:
                        val, ok = hi, vc & ~first
                    elif hi is None:
                        val, ok = lo, vc & first
                    else:
                        val = jnp.where(first, lo, hi)
                    tile = jnp.where(ok, val * LOG2E, NEG_INF)
                tbl_ref[t, i * GRID_W:(i + 1) * GRID_W, kp * LANES:(kp + 1) * LANES] = tile


def _natten_kernel(q_ref, k_ref, v_ref, g_ref, rpb_ref, o_ref, tbl_ref, v1_ref, *, rows):
    qb = NA_QB_ROWS * GRID_W
    span = NA_UNION_ROWS * GRID_W
    nb_step = q_ref.shape[0] // qb
    types, block_type = _natten_geometry(rows)
    common = max(set(block_type), key=block_type.count)
    step = pl.program_id(1)

    @pl.when(step == 0)
    def _():
        _build_natten_tables(rpb_ref, tbl_ref, types)
        v1_ref[:, :LANES] = v_ref[...]
        v1_ref[:, LANES:] = jnp.ones((v_ref.shape[0], LANES), BF16)

    for b in range(nb_step):
        blk = step * nb_step + b
        u0 = jnp.clip(blk * NA_QB_ROWS - NA_ROWS // 2, 0, rows - NA_UNION_ROWS)
        typ = common
        for bb, t in enumerate(block_type):
            if t != common:
                typ = jnp.where(blk == bb, t, typ)
        koff = pl.multiple_of(u0 * GRID_W, GRID_W)
        qs = slice(b * qb, (b + 1) * qb)
        k = k_ref[pl.ds(koff, span), :]
        v1 = v1_ref[pl.ds(koff, span), :]
        s = lax.dot_general(q_ref[qs, :], k, (((1,), (1,)), ((), ())), preferred_element_type=F32)
        s = s + tbl_ref[typ]
        m = jnp.max(s, axis=-1, keepdims=True)
        pv = jnp.dot(jnp.exp2(s - m).astype(BF16), v1, preferred_element_type=F32)
        o_ref[qs, :] = (pv[:, :LANES] / pv[:, LANES:] * g_ref[qs, :]).astype(o_ref.dtype)


def _natten(q, k, v, gate, gate_col0, rpb, cfg):
    (q_arr, q_col0), (k_arr, k_col0), (v_arr, v_col0) = q, k, v
    s_len = q_arr.shape[0]
    rows = s_len // GRID_W
    tq = min(cfg.tq_b, s_len)
    qb = NA_QB_ROWS * GRID_W
    span = NA_UNION_ROWS * GRID_W
    assert rows % NA_QB_ROWS == 0 and rows >= NA_UNION_ROWS and tq % qb == 0
    n_types = len(_natten_geometry(rows)[0])
    nh, nr, nc = rpb.shape
    rpb_pad = jnp.pad(rpb.astype(F32), ((0, 0), (0, 2 * NA_ROWS - nr), (0, LANES - nc)))
    return pl.pallas_call(
        functools.partial(_natten_kernel, rows=rows),
        grid=(cfg.b_heads, s_len // tq),
        in_specs=[pl.BlockSpec((tq, LANES), lambda h, i: (i, q_col0 // LANES + h)),
                  pl.BlockSpec((s_len, LANES), lambda h, i: (0, k_col0 // LANES + h)),
                  pl.BlockSpec((s_len, LANES), lambda h, i: (0, v_col0 // LANES + h)),
                  pl.BlockSpec((tq, LANES), lambda h, i: (i, gate_col0 // LANES + h)),
                  pl.BlockSpec((None, 2 * NA_ROWS, LANES), lambda h, i: (h, 0, 0))],
        out_specs=pl.BlockSpec((tq, LANES), lambda h, i: (i, h)),
        out_shape=jax.ShapeDtypeStruct((s_len, cfg.b_heads * LANES), BF16),
        scratch_shapes=[pltpu.VMEM((n_types, qb, span), F32),
                        pltpu.VMEM((s_len, 2 * LANES), BF16)],
        compiler_params=_params("parallel", "arbitrary"),
        name="attn_neighbourhood",
    )(q_arr, k_arr, v_arr, gate, rpb_pad)


def _attn_c_kernel(slope_ref, sink_ref, q_ref, k_ref, v_ref, g_ref, o_ref, v1_ref, bias_ref, *, group, sub):
    tq = q_ref.shape[0]
    s_len = k_ref.shape[0]
    span = sub + 2 * C_WINDOW
    kv = pl.program_id(0)

    n_place = 3

    @pl.when(pl.program_id(1) == 0)
    def _():
        v1_ref[:, :LANES] = v_ref[...]
        v1_ref[:, LANES:] = jnp.ones((s_len, LANES), BF16)
        row = lax.broadcasted_iota(jnp.int32, (sub, span), 0)
        col = lax.broadcasted_iota(jnp.int32, (sub, span), 1)
        for place in range(n_place):
            dist = jnp.abs(row - col + place * C_WINDOW)
            pen = jnp.where(dist <= C_WINDOW, dist.astype(F32), -NEG_INF)
            for g in range(group):
                bias_ref[place * group + g] = -(slope_ref[kv * group + g] * LOG2E) * pen

    sinks = [sink_ref[kv * group + g] * LOG2E for g in range(group)]
    for b in range(tq // sub):
        t0 = pl.program_id(1) * tq + b * sub
        start = pl.multiple_of(jnp.clip(t0 - C_WINDOW, 0, s_len - span), C_WINDOW)
        place = (t0 - start) // C_WINDOW
        k = k_ref[pl.ds(start, span), :]
        v1 = v1_ref[pl.ds(start, span), :]
        rows = slice(b * sub, (b + 1) * sub)
        q = jnp.concatenate([q_ref[rows, g * LANES:(g + 1) * LANES] for g in range(group)], axis=0)
        s = lax.dot_general(q, k, (((1,), (1,)), ((), ())), preferred_element_type=F32)
        ps, ms = [], []
        for g in range(group):
            sg = s[g * sub:(g + 1) * sub, :] + bias_ref[place * group + g]
            m = jnp.maximum(jnp.max(sg, axis=-1, keepdims=True), sinks[g])
            ps.append(jnp.exp2(sg - m).astype(BF16))
            ms.append(m)
        pv = jnp.dot(jnp.concatenate(ps, axis=0), v1, preferred_element_type=F32)
        for g in range(group):
            sl = slice(g * LANES, (g + 1) * LANES)
            pg = pv[g * sub:(g + 1) * sub, :]
            l = pg[:, LANES:] + jnp.exp2(sinks[g] - ms[g])
            o_ref[rows, sl] = (pg[:, :LANES] / l * g_ref[rows, sl]).astype(o_ref.dtype)


def _attn_c(q, k, v, gate, slopes, sink, cfg):
    (q_arr, q_col0), (k_arr, k_col0), (v_arr, v_col0) = q, k, v
    s_len = q_arr.shape[0]
    group = cfg.c_heads // cfg.c_kv_heads
    gw = group * LANES
    tq, sub = cfg.tq_c, cfg.sub_c
    assert s_len >= sub + 2 * C_WINDOW and s_len % tq == 0 and tq % sub == 0 and sub % C_WINDOW == 0
    smem = pl.BlockSpec(memory_space=pltpu.SMEM)
    return pl.pallas_call(
        functools.partial(_attn_c_kernel, group=group, sub=sub),
        grid=(cfg.c_kv_heads, s_len // tq),
        scratch_shapes=[pltpu.VMEM((s_len, 2 * LANES), BF16),
                        pltpu.VMEM((3 * group, sub, sub + 2 * C_WINDOW), F32)],
        in_specs=[smem, smem,
                  pl.BlockSpec((tq, gw), lambda kv, i: (i, q_col0 // gw + kv)),
                  pl.BlockSpec((s_len, LANES), lambda kv, i: (0, k_col0 // LANES + kv)),
                  pl.BlockSpec((s_len, LANES), lambda kv, i: (0, v_col0 // LANES + kv)),
                  pl.BlockSpec((tq, gw), lambda kv, i: (i, kv))],
        out_specs=pl.BlockSpec((tq, gw), lambda kv, i: (i, kv)),
        out_shape=jax.ShapeDtypeStruct((s_len, cfg.c_heads * LANES), BF16),
        compiler_params=_params("parallel", "arbitrary"),
        name="attn_window",
    )(slopes, sink, q_arr, k_arr, v_arr, gate)


def _rope_tables(s_len):
    half = LANES // 2
    n_rows = s_len // GRID_W
    inv = jnp.exp(-math.log(ROPE_THETA) * jnp.arange(0, half, 2, dtype=F32) / half)
    ang_r = jnp.arange(n_rows, dtype=F32)[:, None] * inv[None, :]
    ang_c = jnp.arange(GRID_W, dtype=F32)[:, None] * inv[None, :]
    zeros_r, zeros_c = jnp.zeros_like(ang_r), jnp.zeros_like(ang_c)

    def table(fr, fc):
        r = jnp.broadcast_to(fr[:, None, :], (n_rows, GRID_W, half))
        c = jnp.broadcast_to(fc[None, :, :], (n_rows, GRID_W, half))
        return jnp.concatenate([r, c], axis=-1).reshape(s_len, LANES)

    pair = lambda t: jnp.concatenate([t, t], axis=1)
    lo = lambda t, z: jnp.concatenate([-t, z], axis=1)
    hi = lambda t, z: jnp.concatenate([z, t], axis=1)
    cos_r, sin_r, cos_c, sin_c = jnp.cos(ang_r), jnp.sin(ang_r), jnp.cos(ang_c), jnp.sin(ang_c)
    return (table(pair(cos_r), pair(cos_c)), table(lo(sin_r, zeros_r), lo(sin_c, zeros_c)),
            table(hi(sin_r, zeros_r), hi(sin_c, zeros_c)))


def _forward(x, norm_w, w_in_ab, w_out_ab, q_norm_a, k_norm_a, rpb_b, w_in_c, w_out_c, sink_c,
             final_norm_w, cfg):
    bsz, s_len, d = x.shape
    assert bsz == 1 and s_len % GRID_W == 0
    depth = norm_w.shape[0]
    scale = LANES ** -0.5 * LOG2E
    a_q, a_kv, b_w = cfg.a_heads * LANES, cfg.a_kv_heads * LANES, cfg.b_heads * LANES
    c_q, c_kv = cfg.c_heads * LANES, cfg.c_kv_heads * LANES
    rope = _rope_tables(s_len)
    slopes = jnp.exp2(-8.0 * jnp.arange(1, cfg.c_heads + 1, dtype=F32) / cfg.c_heads)
    xs = x.reshape(s_len, d)
    h, ssq = _prep(xs, cfg.t_norm)
    for layer in range(depth):
        lnw = norm_w[layer]
        if layer % 2 == 0:
            e = layer // 2
            n_rope = a_q + a_kv
            zr = _rope_proj(h, ssq, lnw, w_in_ab, e, a_q, a_kv, scale, rope, q_norm_a[e], k_norm_a[e], cfg)
            zc = _cast_proj(h, ssq, lnw, w_in_ab, e, n_rope, ((a_kv, 1.0), (b_w, scale), (2 * b_w, 1.0)),
                            cfg)
            gate = _gate_proj(h, ssq, lnw, w_in_ab, e, n_rope + a_kv + 3 * b_w, a_q + b_w, cfg)
            ya = _attn_a((zr, 0), (zr, a_q), (zc, 0), gate, cfg)
            yb = _natten((zc, a_kv), (zc, a_kv + b_w), (zc, a_kv + 2 * b_w), gate, a_q, rpb_b[e], cfg)
            xs, h, ssq = _outproj([ya, yb], w_out_ab, e, xs, cfg, layer + 1 < depth)
        else:
            o = layer // 2
            zc = _cast_proj(h, ssq, lnw, w_in_c, o, 0, ((c_q, scale), (2 * c_kv, 1.0)), cfg)
            gate = _gate_proj(h, ssq, lnw, w_in_c, o, c_q + 2 * c_kv, c_q, cfg)
            y = _attn_c((zc, 0), (zc, c_q), (zc, c_q + c_kv), gate, slopes, sink_c[o].astype(F32), cfg)
            xs, h, ssq = _outproj([y], w_out_c, o, xs, cfg, layer + 1 < depth)
    out = _rmsnorm(xs, final_norm_w, x.dtype, cfg.t_norm)
    return out.reshape(bsz, s_len, d)


def kernel(x, norm_w, w_in_ab, w_out_ab, q_norm_a, k_norm_a, rpb_b, w_in_c, w_out_c, sink_c, final_norm_w):
    return _forward(x, norm_w, w_in_ab, w_out_ab, q_norm_a, k_norm_a, rpb_b, w_in_c, w_out_c, sink_c,
                    final_norm_w, Config())
```

```python
import functools
import math
from typing import NamedTuple

import jax
import jax.numpy as jnp
import numpy as np
from jax import lax
from jax.experimental import pallas as pl
from jax.experimental.pallas import tpu as pltpu

F32 = jnp.float32
BF16 = jnp.bfloat16

LANES = 128
NORM_EPS = 1e-6
NEG_INF = -1e30
LOG2E = math.log2(math.e)
ROPE_THETA = 10000.0
GRID_W = 64
NA_ROWS = 8
NA_COLS = 16
NA_QB_ROWS = 4
NA_UNION_ROWS = 12
C_WINDOW = 128
PROJ_SPLIT = 8
CAST_SPLIT = 4
A_SAFE_SHIFT = 60.0
VMEM_LIMIT_BYTES = 56 * 1024 * 1024


class Config(NamedTuple):
    a_heads: int = 16
    a_kv_heads: int = 4
    b_heads: int = 16
    c_heads: int = 32
    c_kv_heads: int = 8
    tm: int = 1024
    tn: int = 512
    tq_a: int = 256
    tk_a: int = 1024
    chunk_a: int = 16
    tq_c: int = 4096
    sub_c: int = 128
    tq_b: int = 8192
    t_norm: int = 256


def _params(*sem):
    return pltpu.CompilerParams(dimension_semantics=sem, vmem_limit_bytes=VMEM_LIMIT_BYTES)


def _rmsnorm_kernel(x_ref, w_ref, o_ref):
    x = x_ref[...]
    ms = jnp.mean(x * x, axis=-1, keepdims=True)
    o_ref[...] = (x * lax.rsqrt(ms + NORM_EPS) * w_ref[...]).astype(o_ref.dtype)


def _rmsnorm(x, w, out_dtype, t):
    m, d = x.shape
    return pl.pallas_call(
        _rmsnorm_kernel,
        grid=(m // t,),
        in_specs=[pl.BlockSpec((t, d), lambda i: (i, 0)),
                  pl.BlockSpec((1, d), lambda i: (0, 0))],
        out_specs=pl.BlockSpec((t, d), lambda i: (i, 0)),
        out_shape=jax.ShapeDtypeStruct((m, d), out_dtype),
        compiler_params=_params("parallel"),
        name="rmsnorm",
    )(x, w.reshape(1, d))


def _lane_partial_sumsq(x):
    sq = x * x
    acc = sq[:, :LANES]
    for c in range(1, x.shape[1] // LANES):
        acc = acc + sq[:, c * LANES:(c + 1) * LANES]
    return acc


def _prep_kernel(x_ref, xb_ref, ssq_ref):
    x = x_ref[...]
    xb_ref[...] = x.astype(xb_ref.dtype)
    ssq_ref[...] = _lane_partial_sumsq(x)


def _prep(x, t):
    m, d = x.shape
    return pl.pallas_call(
        _prep_kernel,
        grid=(m // t,),
        in_specs=[pl.BlockSpec((t, d), lambda i: (i, 0))],
        out_specs=[pl.BlockSpec((t, d), lambda i: (i, 0)), pl.BlockSpec((t, LANES), lambda i: (i, 0))],
        out_shape=[jax.ShapeDtypeStruct((m, d), BF16), jax.ShapeDtypeStruct((m, LANES), F32)],
        compiler_params=_params("parallel"),
        name="prep",
    )(x)


def _weight_copy(w_hbm, wbuf_ref, sem_ref, e, tile, slot):
    tn = wbuf_ref.shape[2]
    cols = pl.ds(pl.multiple_of(tile * tn, tn), tn)
    return pltpu.make_async_copy(w_hbm.at[e, :, cols], wbuf_ref.at[slot], sem_ref.at[slot])


def _stage_weight(w_hbm, wbuf_ref, sem_ref, wbf_ref, e, tile0, row_scale_ref=None):
    j, nj = pl.program_id(0), pl.num_programs(0)

    @pl.when(pl.program_id(1) == 0)
    def _():
        slot = j % 2

        @pl.when(j == 0)
        def _():
            _weight_copy(w_hbm, wbuf_ref, sem_ref, e, tile0, 0).start()

        @pl.when(j + 1 < nj)
        def _():
            _weight_copy(w_hbm, wbuf_ref, sem_ref, e, tile0 + j + 1, 1 - slot).start()

        _weight_copy(w_hbm, wbuf_ref, sem_ref, e, tile0 + j, slot).wait()
        w = wbuf_ref[slot]
        wbf_ref[...] = (w if row_scale_ref is None else w * row_scale_ref[...]).astype(BF16)


def _weight_scratch(k, tn):
    return [pltpu.VMEM((2, k, tn), F32), pltpu.SemaphoreType.DMA((2,)), pltpu.VMEM((k, tn), BF16)]


W_HBM_SPEC = pl.BlockSpec(memory_space=pl.ANY)


def _norm_rope(z, w, cos, sin_lo, sin_hi):
    ms = jnp.mean(z * z, axis=-1, keepdims=True)
    y = z * lax.rsqrt(ms + NORM_EPS) * w
    quarter = LANES // 4
    return (y * cos + pltpu.roll(y, LANES - quarter, 1) * sin_lo
            + pltpu.roll(y, quarter, 1) * sin_hi)


def _rinv(ssq_ref, d):
    return lax.rsqrt(jnp.sum(ssq_ref[...], axis=-1, keepdims=True) * (1.0 / d) + NORM_EPS)


def _row_blocks(tm, split=PROJ_SPLIT):
    step = tm // split
    return [slice(r * step, (r + 1) * step) for r in range(split)]


def _rope_proj_kernel(h_ref, ssq_ref, lnw_ref, w_hbm, cos_ref, slo_ref, shi_ref, qn_ref, kn_ref, o_ref,
                      wbuf_ref, sem_ref, wbf_ref, *, e, q_tiles, q_scale):
    _stage_weight(w_hbm, wbuf_ref, sem_ref, wbf_ref, e, 0, lnw_ref)
    nw = jnp.where(pl.program_id(0) < q_tiles, qn_ref[...] * q_scale, kn_ref[...])
    rinv = _rinv(ssq_ref, h_ref.shape[1])
    for rs in _row_blocks(h_ref.shape[0]):
        acc = jnp.dot(h_ref[rs, :], wbf_ref[...], preferred_element_type=F32) * rinv[rs, :]
        cos, slo, shi = cos_ref[rs, :], slo_ref[rs, :], shi_ref[rs, :]
        for hh in range(acc.shape[1] // LANES):
            sl = slice(hh * LANES, (hh + 1) * LANES)
            o_ref[rs, sl] = _norm_rope(acc[:, sl], nw, cos, slo, shi).astype(o_ref.dtype)


def _rope_proj(h, ssq, lnw, w, e, n_q, n_k, q_scale, rope, qn, kn, cfg):
    m, k = h.shape
    tm, tn = min(cfg.tm, m), cfg.tn
    row = lambda j, i: (i, 0)
    vec = pl.BlockSpec((1, LANES), lambda j, i: (0, 0))
    return pl.pallas_call(
        functools.partial(_rope_proj_kernel, e=e, q_tiles=n_q // tn, q_scale=q_scale),
        grid=((n_q + n_k) // tn, m // tm),
        in_specs=[pl.BlockSpec((tm, k), row), pl.BlockSpec((tm, LANES), row),
                  pl.BlockSpec((k, 1), lambda j, i: (0, 0)), W_HBM_SPEC,
                  pl.BlockSpec((tm, LANES), row), pl.BlockSpec((tm, LANES), row),
                  pl.BlockSpec((tm, LANES), row), vec, vec],
        out_specs=pl.BlockSpec((tm, tn), lambda j, i: (i, j)),
        out_shape=jax.ShapeDtypeStruct((m, n_q + n_k), BF16),
        scratch_shapes=_weight_scratch(k, tn),
        compiler_params=_params("arbitrary", "arbitrary"),
        name="ropeproj",
    )(h, ssq, lnw.reshape(k, 1), w, *rope, qn.reshape(1, LANES), kn.reshape(1, LANES))


def _cast_proj_kernel(scale_ref, h_ref, ssq_ref, lnw_ref, w_hbm, o_ref, wbuf_ref, sem_ref, wbf_ref, *,
                      e, tile0):
    _stage_weight(w_hbm, wbuf_ref, sem_ref, wbf_ref, e, tile0, lnw_ref)
    scale = _rinv(ssq_ref, h_ref.shape[1]) * scale_ref[pl.program_id(0)]
    for rs in _row_blocks(h_ref.shape[0], CAST_SPLIT):
        acc = jnp.dot(h_ref[rs, :], wbf_ref[...], preferred_element_type=F32)
        o_ref[rs, :] = (acc * scale[rs, :]).astype(o_ref.dtype)


def _cast_proj(h, ssq, lnw, w, e, col0, widths_scales, cfg):
    m, k = h.shape
    tm, tn = min(cfg.tm, m), cfg.tn
    scales = np.concatenate([np.full(width // tn, scale, np.float32) for width, scale in widths_scales])
    n_out = tn * len(scales)
    return pl.pallas_call(
        functools.partial(_cast_proj_kernel, e=e, tile0=col0 // tn),
        grid=(n_out // tn, m // tm),
        in_specs=[pl.BlockSpec(memory_space=pltpu.SMEM),
                  pl.BlockSpec((tm, k), lambda j, i: (i, 0)),
                  pl.BlockSpec((tm, LANES), lambda j, i: (i, 0)),
                  pl.BlockSpec((k, 1), lambda j, i: (0, 0)), W_HBM_SPEC],
        out_specs=pl.BlockSpec((tm, tn), lambda j, i: (i, j)),
        out_shape=jax.ShapeDtypeStruct((m, n_out), BF16),
        scratch_shapes=_weight_scratch(k, tn),
        compiler_params=_params("arbitrary", "arbitrary"),
        name="castproj",
    )(jnp.asarray(scales), h, ssq, lnw.reshape(k, 1), w)


def _gate_kernel(h_ref, ssq_ref, lnw_ref, w_hbm, o_ref, wbuf_ref, sem_ref, wbf_ref, *, e, tile0):
    _stage_weight(w_hbm, wbuf_ref, sem_ref, wbf_ref, e, tile0, lnw_ref)
    rinv = _rinv(ssq_ref, h_ref.shape[1])
    for rs in _row_blocks(h_ref.shape[0]):
        z = jnp.dot(h_ref[rs, :], wbf_ref[...], preferred_element_type=F32) * rinv[rs, :]
        o_ref[rs, :] = (z * (0.5 * jnp.tanh(0.5 * z) + 0.5)).astype(o_ref.dtype)


def _gate_proj(h, ssq, lnw, w, e, col0, n_out, cfg):
    m, k = h.shape
    tm, tn = min(cfg.tm, m), cfg.tn
    return pl.pallas_call(
        functools.partial(_gate_kernel, e=e, tile0=col0 // tn),
        grid=(n_out // tn, m // tm),
        in_specs=[pl.BlockSpec((tm, k), lambda j, i: (i, 0)),
                  pl.BlockSpec((tm, LANES), lambda j, i: (i, 0)),
                  pl.BlockSpec((k, 1), lambda j, i: (0, 0)), W_HBM_SPEC],
        out_specs=pl.BlockSpec((tm, tn), lambda j, i: (i, j)),
        out_shape=jax.ShapeDtypeStruct((m, n_out), BF16),
        scratch_shapes=_weight_scratch(k, tn),
        compiler_params=_params("arbitrary", "arbitrary"),
        name="gateproj",
    )(h, ssq, lnw.reshape(k, 1), w)


def _outproj_kernel(*refs, n_lhs, e, feed_next):
    y_refs = refs[:n_lhs]
    if feed_next:
        w_hbm, x_ref, o_ref, ob_ref, ssq_hbm, wbuf_ref, sem_ref, wbf_ref, ssq_acc_ref, ssq_sem = refs[n_lhs:]
    else:
        w_hbm, x_ref, o_ref, wbuf_ref, sem_ref, wbf_ref = refs[n_lhs:]
    _stage_weight(w_hbm, wbuf_ref, sem_ref, wbf_ref, e, 0)
    acc = x_ref[...]
    k0 = 0
    for y_ref in y_refs:
        kk = y_ref.shape[1]
        acc = acc + jnp.dot(y_ref[...], wbf_ref[k0:k0 + kk, :], preferred_element_type=F32)
        k0 += kk
    o_ref[...] = acc
    if not feed_next:
        return
    ob_ref[...] = acc.astype(ob_ref.dtype)
    tm = acc.shape[0]
    rows = pl.ds(pl.multiple_of(pl.program_id(1) * tm, tm), tm)
    part = _lane_partial_sumsq(acc)

    @pl.when(pl.program_id(0) == 0)
    def _():
        ssq_acc_ref[rows, :] = part

    @pl.when(pl.program_id(0) > 0)
    def _():
        ssq_acc_ref[rows, :] += part

    @pl.when(pl.program_id(0) == pl.num_programs(0) - 1)
    def _():
        copy = pltpu.make_async_copy(ssq_acc_ref.at[rows, :], ssq_hbm.at[rows, :], ssq_sem.at[0])
        copy.start()
        copy.wait()


def _outproj(ys, w, e, x, cfg, feed_next):
    m, n = x.shape
    k = w.shape[1]
    tm, tn = min(cfg.tm, m), cfg.tn
    tile = pl.BlockSpec((tm, tn), lambda j, i: (i, j))
    in_specs = [pl.BlockSpec((tm, y.shape[1]), lambda j, i: (i, 0)) for y in ys]
    in_specs += [W_HBM_SPEC, tile]
    out_specs, out_shape, scratch = [tile], [jax.ShapeDtypeStruct((m, n), F32)], _weight_scratch(k, tn)
    if feed_next:
        out_specs += [tile, pl.BlockSpec(memory_space=pl.ANY)]
        out_shape += [jax.ShapeDtypeStruct((m, n), BF16), jax.ShapeDtypeStruct((m, LANES), F32)]
        scratch += [pltpu.VMEM((m, LANES), F32), pltpu.SemaphoreType.DMA((1,))]
    outs = pl.pallas_call(
        functools.partial(_outproj_kernel, n_lhs=len(ys), e=e, feed_next=feed_next),
        grid=(n // tn, m // tm),
        in_specs=in_specs,
        out_specs=out_specs,
        out_shape=out_shape,
        scratch_shapes=scratch,
        compiler_params=_params("arbitrary", "arbitrary"),
        name="outproj",
    )(*ys, w, x)
    return tuple(outs) if feed_next else (outs[0], None, None)


def _attn_a_kernel(q_ref, k_ref, v_ref, g_ref, o_ref, qs_ref, v1_ref, s_ref, p_ref, m_ref, alpha_ref,
                   acc_ref, kmax_ref, *, tk, group, chunk):
    tq = q_ref.shape[0]
    s_len = k_ref.shape[0]
    nk = s_len // tk

    @pl.when(pl.program_id(1) == 0)
    def _():
        v1_ref[:, :LANES] = v_ref[...]
        v1_ref[:, LANES:] = jnp.ones((s_len, LANES), BF16)
        kf = k_ref[...].astype(F32)
        kmax_ref[0] = jnp.max(jnp.sum(kf * kf, axis=-1, keepdims=True))

    for g in range(group):
        qs_ref[g * tq:(g + 1) * tq, :] = q_ref[:, g * LANES:(g + 1) * LANES]
    rows = group * tq

    def scores(t):
        k = k_ref[t * tk:(t + 1) * tk, :]
        return lax.dot_general(qs_ref[...], k, (((1,), (1,)), ((), ())), preferred_element_type=F32)

    qf = qs_ref[...].astype(F32)
    bound = jnp.sqrt(jnp.sum(qf * qf, axis=-1, keepdims=True) * kmax_ref[0])

    def fixed_shift():
        acc_ref[...] = jnp.zeros((rows, 2 * LANES), F32)
        for t in range(nk):
            p = jnp.exp2(scores(t) - bound).astype(BF16)
            acc_ref[...] += jnp.dot(p, v1_ref[t * tk:(t + 1) * tk, :], preferred_element_type=F32)

    def running_max():
        m_ref[...] = jnp.full((rows, 1), NEG_INF, F32)
        acc_ref[...] = jnp.zeros((rows, 2 * LANES), F32)
        s_ref[0] = scores(0)
        for t in range(nk):
            if t + 1 < nk:
                s_ref[(t + 1) % 2] = scores(t + 1)
            for c in range(rows // chunk):
                rs = slice(c * chunk, (c + 1) * chunk)
                s = s_ref[t % 2, rs, :]
                m_old = m_ref[rs, :]
                m_new = jnp.maximum(m_old, jnp.max(s, axis=-1, keepdims=True))
                m_ref[rs, :] = m_new
                alpha_ref[rs, :] = jnp.exp2(m_old - m_new)
                p_ref[t % 2, rs, :] = jnp.exp2(s - m_new).astype(BF16)
            pv = jnp.dot(p_ref[t % 2], v1_ref[t * tk:(t + 1) * tk, :], preferred_element_type=F32)
            acc_ref[...] = alpha_ref[...] * acc_ref[...] + pv

    lax.cond(jnp.max(bound) <= A_SAFE_SHIFT, fixed_shift, running_max)
    o = acc_ref[:, :LANES] / acc_ref[:, LANES:]
    for g in range(group):
        sl = slice(g * LANES, (g + 1) * LANES)
        o_ref[:, sl] = (o[g * tq:(g + 1) * tq, :] * g_ref[:, sl]).astype(o_ref.dtype)


def _attn_a(q, k, v, gate, cfg):
    (q_arr, q_col0), (k_arr, k_col0), (v_arr, v_col0) = q, k, v
    s_len = q_arr.shape[0]
    group = cfg.a_heads // cfg.a_kv_heads
    gw = group * LANES
    tq = min(cfg.tq_a, s_len)
    tk = min(cfg.tk_a, s_len)
    return pl.pallas_call(
        functools.partial(_attn_a_kernel, tk=tk, group=group, chunk=cfg.chunk_a),
        grid=(cfg.a_kv_heads, s_len // tq),
        scratch_shapes=[pltpu.VMEM((group * tq, LANES), BF16),
                        pltpu.VMEM((s_len, 2 * LANES), BF16),
                        pltpu.VMEM((2, group * tq, tk), F32),
                        pltpu.VMEM((2, group * tq, tk), BF16),
                        pltpu.VMEM((group * tq, 1), F32),
                        pltpu.VMEM((group * tq, 1), F32),
                        pltpu.VMEM((group * tq, 2 * LANES), F32),
                        pltpu.SMEM((1,), F32)],
        in_specs=[pl.BlockSpec((tq, gw), lambda kv, i: (i, q_col0 // gw + kv)),
                  pl.BlockSpec((s_len, LANES), lambda kv, i: (0, k_col0 // LANES + kv)),
                  pl.BlockSpec((s_len, LANES), lambda kv, i: (0, v_col0 // LANES + kv)),
                  pl.BlockSpec((tq, gw), lambda kv, i: (i, kv))],
        out_specs=pl.BlockSpec((tq, gw), lambda kv, i: (i, kv)),
        out_shape=jax.ShapeDtypeStruct((s_len, cfg.a_heads * LANES), BF16),
        compiler_params=_params("parallel", "arbitrary"),
        name="attn_global",
    )(q_arr, k_arr, v_arr, gate)


def _natten_geometry(rows):
    types, block_type = [], []
    for b in range(rows // NA_QB_ROWS):
        r = b * NA_QB_ROWS + np.arange(NA_QB_ROWS)
        u0 = int(np.clip(r[0] - NA_ROWS // 2, 0, rows - NA_UNION_ROWS))
        rs = np.clip(r - NA_ROWS // 2, 0, rows - NA_ROWS)
        assert u0 <= rs.min() and rs.max() + NA_ROWS <= u0 + NA_UNION_ROWS
        kr = u0 + np.arange(NA_UNION_ROWS)
        vr = (kr[None, :] >= rs[:, None]) & (kr[None, :] < rs[:, None] + NA_ROWS)
        dr = np.where(vr, kr[None, :] - r[:, None] + (NA_ROWS - 1), 0)
        for t, (dr_t, vr_t) in enumerate(types):
            if np.array_equal(dr, dr_t) and np.array_equal(vr, vr_t):
                block_type.append(t)
                break
        else:
            block_type.append(len(types))
            types.append((dr, vr))
    return types, block_type


def _build_natten_tables(rpb_ref, tbl_ref, types):
    c = lax.broadcasted_iota(jnp.int32, (GRID_W, LANES), 0)
    lane = lax.broadcasted_iota(jnp.int32, (GRID_W, LANES), 1)
    first = lane < GRID_W
    kc = jnp.where(first, lane, lane - GRID_W)
    cs = jnp.clip(c - NA_COLS // 2, 0, GRID_W - NA_COLS)
    vc = (kc >= cs) & (kc < cs + NA_COLS)
    base_shift = LANES - (NA_COLS - 1)
    for t, (dr, vr) in enumerate(types):
        for i in range(NA_QB_ROWS):
            for kp in range(NA_UNION_ROWS // 2):
                halves = []
                for half in range(2):
                    ku = 2 * kp + half
                    if vr[i, ku]:
                        row = jnp.broadcast_to(rpb_ref[int(dr[i, ku]):int(dr[i, ku]) + 1, :],
                                               (GRID_W, LANES))
                        halves.append(pltpu.roll(row, (base_shift + half * GRID_W) % LANES, 1,
                                                 stride=1, stride_axis=0))
                    else:
                        halves.append(None)
                lo, hi = halves
                if lo is None and hi is None:
                    tile = jnp.full((GRID_W, LANES), NEG_INF, F32)
                else:
                    ok = vc
                    if lo is None:
                        val, ok = hi, vc & ~first
                    elif hi is None:
                        val, ok = lo, vc & first
                    else:
                        val = jnp.where(first, lo, hi)
                    tile = jnp.where(ok, val * LOG2E, NEG_INF)
                tbl_ref[t, i * GRID_W:(i + 1) * GRID_W, kp * LANES:(kp + 1) * LANES] = tile


def _natten_kernel(q_ref, k_ref, v_ref, g_ref, rpb_ref, o_ref, tbl_ref, v1_ref, *, rows):
    qb = NA_QB_ROWS * GRID_W
    span = NA_UNION_ROWS * GRID_W
    nb_step = q_ref.shape[0] // qb
    types, block_type = _natten_geometry(rows)
    common = max(set(block_type), key=block_type.count)
    step = pl.program_id(1)

    @pl.when(step == 0)
    def _():
        _build_natten_tables(rpb_ref, tbl_ref, types)
        v1_ref[:, :LANES] = v_ref[...]
        v1_ref[:, LANES:] = jnp.ones((v_ref.shape[0], LANES), BF16)

    for b in range(nb_step):
        blk = step * nb_step + b
        u0 = jnp.clip(blk * NA_QB_ROWS - NA_ROWS // 2, 0, rows - NA_UNION_ROWS)
        typ = common
        for bb, t in enumerate(block_type):
            if t != common:
                typ = jnp.where(blk == bb, t, typ)
        koff = pl.multiple_of(u0 * GRID_W, GRID_W)
        qs = slice(b * qb, (b + 1) * qb)
        k = k_ref[pl.ds(koff, span), :]
        v1 = v1_ref[pl.ds(koff, span), :]
        s = lax.dot_general(q_ref[qs, :], k, (((1,), (1,)), ((), ())), preferred_element_type=F32)
        s = s + tbl_ref[typ]
        m = jnp.max(s, axis=-1, keepdims=True)
        pv = jnp.dot(jnp.exp2(s - m).astype(BF16), v1, preferred_element_type=F32)
        o_ref[qs, :] = (pv[:, :LANES] / pv[:, LANES:] * g_ref[qs, :]).astype(o_ref.dtype)


def _natten(q, k, v, gate, gate_col0, rpb, cfg):
    (q_arr, q_col0), (k_arr, k_col0), (v_arr, v_col0) = q, k, v
    s_len = q_arr.shape[0]
    rows = s_len // GRID_W
    tq = min(cfg.tq_b, s_len)
    qb = NA_QB_ROWS * GRID_W
    span = NA_UNION_ROWS * GRID_W
    assert rows % NA_QB_ROWS == 0 and rows >= NA_UNION_ROWS and tq % qb == 0
    n_types = len(_natten_geometry(rows)[0])
    nh, nr, nc = rpb.shape
    rpb_pad = jnp.pad(rpb.astype(F32), ((0, 0), (0, 2 * NA_ROWS - nr), (0, LANES - nc)))
    return pl.pallas_call(
        functools.partial(_natten_kernel, rows=rows),
        grid=(cfg.b_heads, s_len // tq),
        in_specs=[pl.BlockSpec((tq, LANES), lambda h, i: (i, q_col0 // LANES + h)),
                  pl.BlockSpec((s_len, LANES), lambda h, i: (0, k_col0 // LANES + h)),
                  pl.BlockSpec((s_len, LANES), lambda h, i: (0, v_col0 // LANES + h)),
                  pl.BlockSpec((tq, LANES), lambda h, i: (i, gate_col0 // LANES + h)),
                  pl.BlockSpec((None, 2 * NA_ROWS, LANES), lambda h, i: (h, 0, 0))],
        out_specs=pl.BlockSpec((tq, LANES), lambda h, i: (i, h)),
        out_shape=jax.ShapeDtypeStruct((s_len, cfg.b_heads * LANES), BF16),
        scratch_shapes=[pltpu.VMEM((n_types, qb, span), F32),
                        pltpu.VMEM((s_len, 2 * LANES), BF16)],
        compiler_params=_params("parallel", "arbitrary"),
        name="attn_neighbourhood",
    )(q_arr, k_arr, v_arr, gate, rpb_pad)


def _attn_c_kernel(slope_ref, sink_ref, q_ref, k_ref, v_ref, g_ref, o_ref, v1_ref, bias_ref, *, group, sub):
    tq = q_ref.shape[0]
    s_len = k_ref.shape[0]
    span = sub + 2 * C_WINDOW
    kv = pl.program_id(0)

    n_place = 3

    @pl.when(pl.program_id(1) == 0)
    def _():
        v1_ref[:, :LANES] = v_ref[...]
        v1_ref[:, LANES:] = jnp.ones((s_len, LANES), BF16)
        row = lax.broadcasted_iota(jnp.int32, (sub, span), 0)
        col = lax.broadcasted_iota(jnp.int32, (sub, span), 1)
        for place in range(n_place):
            dist = jnp.abs(row - col + place * C_WINDOW)
            pen = jnp.where(dist <= C_WINDOW, dist.astype(F32), -NEG_INF)
            for g in range(group):
                bias_ref[place * group + g] = -(slope_ref[kv * group + g] * LOG2E) * pen

    sinks = [sink_ref[kv * group + g] * LOG2E for g in range(group)]
    for b in range(tq // sub):
        t0 = pl.program_id(1) * tq + b * sub
        start = pl.multiple_of(jnp.clip(t0 - C_WINDOW, 0, s_len - span), C_WINDOW)
        place = (t0 - start) // C_WINDOW
        k = k_ref[pl.ds(start, span), :]
        v1 = v1_ref[pl.ds(start, span), :]
        rows = slice(b * sub, (b + 1) * sub)
        q = jnp.concatenate([q_ref[rows, g * LANES:(g + 1) * LANES] for g in range(group)], axis=0)
        s = lax.dot_general(q, k, (((1,), (1,)), ((), ())), preferred_element_type=F32)
        ps, ms = [], []
        for g in range(group):
            sg = s[g * sub:(g + 1) * sub, :] + bias_ref[place * group + g]
            m = jnp.maximum(jnp.max(sg, axis=-1, keepdims=True), sinks[g])
            ps.append(jnp.exp2(sg - m).astype(BF16))
            ms.append(m)
        pv = jnp.dot(jnp.concatenate(ps, axis=0), v1, preferred_element_type=F32)
        for g in range(group):
            sl = slice(g * LANES, (g + 1) * LANES)
            pg = pv[g * sub:(g + 1) * sub, :]
            l = pg[:, LANES:] + jnp.exp2(sinks[g] - ms[g])
            o_ref[rows, sl] = (pg[:, :LANES] / l * g_ref[rows, sl]).astype(o_ref.dtype)


def _attn_c(q, k, v, gate, slopes, sink, cfg):
    (q_arr, q_col0), (k_arr, k_col0), (v_arr, v_col0) = q, k, v
    s_len = q_arr.shape[0]
    group = cfg.c_heads // cfg.c_kv_heads
    gw = group * LANES
    tq, sub = cfg.tq_c, cfg.sub_c
    assert s_len >= sub + 2 * C_WINDOW and s_len % tq == 0 and tq % sub == 0 and sub % C_WINDOW == 0
    smem = pl.BlockSpec(memory_space=pltpu.SMEM)
    return pl.pallas_call(
        functools.partial(_attn_c_kernel, group=group, sub=sub),
        grid=(cfg.c_kv_heads, s_len // tq),
        scratch_shapes=[pltpu.VMEM((s_len, 2 * LANES), BF16),
                        pltpu.VMEM((3 * group, sub, sub + 2 * C_WINDOW), F32)],
        in_specs=[smem, smem,
                  pl.BlockSpec((tq, gw), lambda kv, i: (i, q_col0 // gw + kv)),
                  pl.BlockSpec((s_len, LANES), lambda kv, i: (0, k_col0 // LANES + kv)),
                  pl.BlockSpec((s_len, LANES), lambda kv, i: (0, v_col0 // LANES + kv)),
                  pl.BlockSpec((tq, gw), lambda kv, i: (i, kv))],
        out_specs=pl.BlockSpec((tq, gw), lambda kv, i: (i, kv)),
        out_shape=jax.ShapeDtypeStruct((s_len, cfg.c_heads * LANES), BF16),
        compiler_params=_params("parallel", "arbitrary"),
        name="attn_window",
    )(slopes, sink, q_arr, k_arr, v_arr, gate)


def _rope_tables(s_len):
    half = LANES // 2
    n_rows = s_len // GRID_W
    inv = jnp.exp(-math.log(ROPE_THETA) * jnp.arange(0, half, 2, dtype=F32) / half)
    ang_r = jnp.arange(n_rows, dtype=F32)[:, None] * inv[None, :]
    ang_c = jnp.arange(GRID_W, dtype=F32)[:, None] * inv[None, :]
    zeros_r, zeros_c = jnp.zeros_like(ang_r), jnp.zeros_like(ang_c)

    def table(fr, fc):
        r = jnp.broadcast_to(fr[:, None, :], (n_rows, GRID_W, half))
        c = jnp.broadcast_to(fc[None, :, :], (n_rows, GRID_W, half))
        return jnp.concatenate([r, c], axis=-1).reshape(s_len, LANES)

    pair = lambda t: jnp.concatenate([t, t], axis=1)
    lo = lambda t, z: jnp.concatenate([-t, z], axis=1)
    hi = lambda t, z: jnp.concatenate([z, t], axis=1)
    cos_r, sin_r, cos_c, sin_c = jnp.cos(ang_r), jnp.sin(ang_r), jnp.cos(ang_c), jnp.sin(ang_c)
    return (table(pair(cos_r), pair(cos_c)), table(lo(sin_r, zeros_r), lo(sin_c, zeros_c)),
            table(hi(sin_r, zeros_r), hi(sin_c, zeros_c)))


def _forward(x, norm_w, w_in_ab, w_out_ab, q_norm_a, k_norm_a, rpb_b, w_in_c, w_out_c, sink_c,
             final_norm_w, cfg):
    bsz, s_len, d = x.shape
    assert bsz == 1 and s_len % GRID_W == 0
    depth = norm_w.shape[0]
    scale = LANES ** -0.5 * LOG2E
    a_q, a_kv, b_w = cfg.a_heads * LANES, cfg.a_kv_heads * LANES, cfg.b_heads * LANES
    c_q, c_kv = cfg.c_heads * LANES, cfg.c_kv_heads * LANES
    rope = _rope_tables(s_len)
    slopes = jnp.exp2(-8.0 * jnp.arange(1, cfg.c_heads + 1, dtype=F32) / cfg.c_heads)
    xs = x.reshape(s_len, d)
    h, ssq = _prep(xs, cfg.t_norm)
    for layer in range(depth):
        lnw = norm_w[layer]
        if layer % 2 == 0:
            e = layer // 2
            n_rope = a_q + a_kv
            zr = _rope_proj(h, ssq, lnw, w_in_ab, e, a_q, a_kv, scale, rope, q_norm_a[e], k_norm_a[e], cfg)
            zc = _cast_proj(h, ssq, lnw, w_in_ab, e, n_rope, ((a_kv, 1.0), (b_w, scale), (2 * b_w, 1.0)),
                            cfg)
            gate = _gate_proj(h, ssq, lnw, w_in_ab, e, n_rope + a_kv + 3 * b_w, a_q + b_w, cfg)
            ya = _attn_a((zr, 0), (zr, a_q), (zc, 0), gate, cfg)
            yb = _natten((zc, a_kv), (zc, a_kv + b_w), (zc, a_kv + 2 * b_w), gate, a_q, rpb_b[e], cfg)
            xs, h, ssq = _outproj([ya, yb], w_out_ab, e, xs, cfg, layer + 1 < depth)
        else:
            o = layer // 2
            zc = _cast_proj(h, ssq, lnw, w_in_c, o, 0, ((c_q, scale), (2 * c_kv, 1.0)), cfg)
            gate = _gate_proj(h, ssq, lnw, w_in_c, o, c_q + 2 * c_kv, c_q, cfg)
            y = _attn_c((zc, 0), (zc, c_q), (zc, c_q + c_kv), gate, slopes, sink_c[o].astype(F32), cfg)
            xs, h, ssq = _outproj([y], w_out_c, o, xs, cfg, layer + 1 < depth)
    out = _rmsnorm(xs, final_norm_w, x.dtype, cfg.t_norm)
    return out.reshape(bsz, s_len, d)


def kernel(x, norm_w, w_in_ab, w_out_ab, q_norm_a, k_norm_a, rpb_b, w_in_c, w_out_c, sink_c, final_norm_w):
    return _forward(x, norm_w, w_in_ab, w_out_ab, q_norm_a, k_norm_a, rpb_b, w_in_c, w_out_c, sink_c,
                    final_norm_w, Config())
```
